```python
import math
import jax, jax.numpy as jnp
from jax import lax
import numpy as np

D_MODEL = 1024
BATCH = 4
SEQ = 8192
DEPTH = 1

BRANCH_W = D_MODEL // 2
POOL_WINDOWS = (2, 4, 8, 16)
N_POOL_GROUPS = len(POOL_WINDOWS)
POOL_GROUP_W = BRANCH_W // N_POOL_GROUPS
DILATED_GROUPS = ((128, 1), (512, 4), (2048, 16))
N_ATT_GROUPS = len(DILATED_GROUPS)
HEAD_DIM = 64
HEADS_PER_GROUP = BRANCH_W // HEAD_DIM
N_ATT_HEADS = N_ATT_GROUPS * HEADS_PER_GROUP
QKV_W = N_ATT_HEADS * HEAD_DIM
Q_BLOCK = 128
NUM_BUCKETS = 32
MAX_DISTANCE = 1024
IN_W = 2 * BRANCH_W + 3 * QKV_W + BRANCH_W + 2 * D_MODEL
RMS_EPS = 1e-6
NEG_INF = -1e30

kernel_name = "hybrid_pool_dilated_attn_gated_block"


def _rmsnorm(x, g):
    xf = x.astype(jnp.float32)
    r = lax.rsqrt(jnp.mean(xf * xf, axis=-1, keepdims=True) + RMS_EPS)
    return (xf * r).astype(x.dtype) * g


def _t5_bucket(rel):
    nb = NUM_BUCKETS // 2
    ret = (rel > 0).astype(jnp.int32) * nb
    n = jnp.abs(rel)
    max_exact = nb // 2
    nf = jnp.maximum(n, 1).astype(jnp.float32)
    large = max_exact + (jnp.log(nf / max_exact) / math.log(MAX_DISTANCE / max_exact)
                         * (nb - max_exact)).astype(jnp.int32)
    large = jnp.minimum(large, nb - 1)
    return ret + jnp.where(n < max_exact, n, large)


def _multiscale_pool(u, w_pool, pool_scale):
    B, S, _ = u.shape
    ug = u.reshape(B, S, N_POOL_GROUPS, POOL_GROUP_W)
    cs = jnp.cumsum(ug.astype(jnp.float32), axis=1)
    cs = jnp.concatenate([jnp.zeros_like(cs[:, :1]), cs], axis=1)
    pos = jnp.arange(S, dtype=jnp.int32)[:, None]
    half = jnp.array([w // 2 for w in POOL_WINDOWS], dtype=jnp.int32)[None, :]
    lo = jnp.maximum(pos - half, 0)
    hi = jnp.minimum(pos + half - 1, S - 1)
    gidx = jnp.arange(N_POOL_GROUPS, dtype=jnp.int32)[None, :]
    win_sum = cs[:, hi + 1, gidx] - cs[:, lo, gidx]
    count = (hi - lo + 1).astype(jnp.float32)[None, :, :, None]
    pooled = (win_sum / count).astype(u.dtype) - ug
    mixed = jnp.einsum('bsgc,gcd->bsgd', pooled, w_pool)
    return mixed.reshape(B, S, BRANCH_W) * pool_scale


def _dilated_attention(q, k, v, rel_bias):
    B, S = q.shape[0], q.shape[1]
    n_blocks = S // Q_BLOCK
    scale = 1.0 / math.sqrt(HEAD_DIM)
    cfg = []
    for g, (window, dil) in enumerate(DILATED_GROUPS):
        n_side = window // (2 * dil)
        offs = jnp.arange(-n_side, n_side + 1, dtype=jnp.int32) * dil
        bias = rel_bias[_t5_bucket(offs)][:, g * HEADS_PER_GROUP:(g + 1) * HEADS_PER_GROUP]
        cfg.append((offs, bias.T.astype(jnp.float32), k[:, :, g], v[:, :, g]))

    def block(i):
        s0 = i * Q_BLOCK
        qpos = s0 + jnp.arange(Q_BLOCK, dtype=jnp.int32)
        qb = lax.dynamic_slice_in_dim(q, s0, Q_BLOCK, axis=1).astype(jnp.float32) * scale
        lses, outs = [], []
        for g, (offs, bias, kg_all, vg_all) in enumerate(cfg):
            nk = offs.shape[0]
            kpos = qpos[:, None] + offs[None, :]
            valid = (kpos >= 0) & (kpos < S)
            kidx = jnp.clip(kpos, 0, S - 1).reshape(-1)
            kg = jnp.take(kg_all, kidx, axis=1).reshape(B, Q_BLOCK, nk, HEADS_PER_GROUP, HEAD_DIM)
            vg = jnp.take(vg_all, kidx, axis=1).reshape(B, Q_BLOCK, nk, HEADS_PER_GROUP, HEAD_DIM)
            s = jnp.einsum('bqhd,bqkhd->bhqk', qb[:, :, g], kg.astype(jnp.float32))
            s = s + bias[None, :, None, :]
            s = jnp.where(valid[None, None], s, NEG_INF)
            lse = jax.nn.logsumexp(s, axis=-1)
            p = jnp.exp(s - lse[..., None])
            outs.append(jnp.einsum('bhqk,bqkhd->bqhd', p, vg.astype(jnp.float32)))
            lses.append(lse)
        wts = jax.nn.softmax(jnp.stack(lses, axis=0), axis=0)
        wts = jnp.transpose(wts, (0, 1, 3, 2))[..., None]
        out = jnp.sum(wts * jnp.stack(outs, axis=0), axis=0)
        return out.astype(q.dtype)

    o = lax.map(block, jnp.arange(n_blocks, dtype=jnp.int32))
    o = jnp.transpose(o, (1, 0, 2, 3, 4))
    return o.reshape(B, S, HEADS_PER_GROUP * HEAD_DIM)


def setup_inputs(seed: int = 0) -> dict:
    key = jax.random.key(seed)
    ks = jax.random.split(key, 12)
    f32 = jnp.float32
    x = jax.random.normal(ks[0], (BATCH, SEQ, D_MODEL), f32)
    norm_gain = 1.0 + 0.05 * jax.random.normal(ks[1], (DEPTH, D_MODEL), f32)
    w_in = jax.random.normal(ks[2], (DEPTH, D_MODEL, IN_W), f32) * D_MODEL ** -0.5
    b_gate = 0.01 * jax.random.normal(ks[3], (DEPTH, 2, D_MODEL), f32)
    rel_bias = 0.5 * jax.random.normal(ks[4], (NUM_BUCKETS, N_ATT_HEADS), f32)
    w_pool = jax.random.normal(ks[5], (DEPTH, N_POOL_GROUPS, POOL_GROUP_W, POOL_GROUP_W), f32) * POOL_GROUP_W ** -0.5
    pool_scale = 1.0 + 0.1 * jax.random.normal(ks[6], (DEPTH, BRANCH_W), f32)
    w_proj_a = jax.random.normal(ks[7], (DEPTH, BRANCH_W, D_MODEL), f32) * BRANCH_W ** -0.5
    w_proj_b = jax.random.normal(ks[8], (DEPTH, BRANCH_W, D_MODEL), f32) * BRANCH_W ** -0.5
    w_out = jax.random.normal(ks[9], (DEPTH, D_MODEL, D_MODEL), f32) * D_MODEL ** -0.5
    final_gain = 1.0 + 0.05 * jax.random.normal(ks[10], (D_MODEL,), f32)
    return {"x": x, "norm_gain": norm_gain, "w_in": w_in, "b_gate": b_gate,
            "rel_bias": rel_bias, "w_pool": w_pool, "pool_scale": pool_scale,
            "w_proj_a": w_proj_a, "w_proj_b": w_proj_b, "w_out": w_out,
            "final_gain": final_gain}


def reference(x, norm_gain, w_in, b_gate, rel_bias, w_pool, pool_scale,
              w_proj_a, w_proj_b, w_out, final_gain):
    B, S, D = x.shape
    splits = np.cumsum([BRANCH_W, BRANCH_W, QKV_W, QKV_W, QKV_W, BRANCH_W]).tolist()
    for l in range(DEPTH):
        h = _rmsnorm(x, norm_gain[l])
        z = h @ w_in[l]
        a_in, a_gate, q, k, v, b_gp, g_logits = jnp.split(z, splits, axis=-1)
        y_a = (_multiscale_pool(a_in, w_pool[l], pool_scale[l]) * jax.nn.silu(a_gate)) @ w_proj_a[l]
        shp = (B, S, N_ATT_GROUPS, HEADS_PER_GROUP, HEAD_DIM)
        att = _dilated_attention(q.reshape(shp), k.reshape(shp), v.reshape(shp), rel_bias)
        y_b = (att * jax.nn.silu(b_gp)) @ w_proj_b[l]
        gates = jax.nn.sigmoid(g_logits.reshape(B, S, 2, D) + b_gate[l])
        merged = gates[:, :, 0] * y_a + gates[:, :, 1] * y_b
        x = x + merged @ w_out[l]
    return _rmsnorm(x, final_gain)
```

```python
import functools
import math

import jax
import jax.numpy as jnp
from jax import lax
from jax.experimental import pallas as pl
from jax.experimental.pallas import tpu as pltpu

D_MODEL = 1024
BRANCH_W = D_MODEL // 2
POOL_WINDOWS = (2, 4, 8, 16)
POOL_GROUP_W = BRANCH_W // len(POOL_WINDOWS)
DILATED_GROUPS = ((128, 1), (512, 4), (2048, 16))
N_ATT_GROUPS = len(DILATED_GROUPS)
HEAD_DIM = 64
HEADS_PER_GROUP = BRANCH_W // HEAD_DIM
QKV_W = N_ATT_GROUPS * BRANCH_W
NUM_BUCKETS = 32
MAX_DISTANCE = 1024
IN_W = 2 * BRANCH_W + 3 * QKV_W + BRANCH_W + 2 * D_MODEL
RMS_EPS = 1e-6
NEG_INF = -1e30

COL_POOL_IN = 0
COL_POOL_GATE = BRANCH_W
COL_Q = 2 * BRANCH_W
COL_K = COL_Q + QKV_W
COL_V = COL_K + QKV_W
COL_ATT_GATE = COL_V + QKV_W
COL_MERGE = COL_ATT_GATE + BRANCH_W

N_SIDE = 64
LANES = 128
Q_SUB = 128
K_SUB = Q_SUB + 2 * N_SIDE
Q_STEP = 512
HALO_ROWS = 16

IN_PROJ_ROWS = 256
IN_PROJ_COLS = 512
EPI_ROWS = 256
VMEM_LIMIT = 56 * 1024 * 1024

_BF16 = jnp.bfloat16
_F32 = jnp.float32


def _in_proj_kernel(x_ref, g_ref, w_ref, z_ref, h_ref):
    x = x_ref[...]
    r = lax.rsqrt(jnp.mean(x * x, axis=-1, keepdims=True) + RMS_EPS)
    h_ref[...] = ((x * r) * g_ref[...]).astype(_BF16)
    for c in range(IN_W // IN_PROJ_COLS):
        cs = slice(c * IN_PROJ_COLS, (c + 1) * IN_PROJ_COLS)
        z_ref[:, cs] = jnp.dot(h_ref[...], w_ref[:, cs],
                               preferred_element_type=_F32).astype(_BF16)


def _in_proj(x2, gain, w_bf):
    n = x2.shape[0]
    return pl.pallas_call(
        _in_proj_kernel,
        grid=(n // IN_PROJ_ROWS,),
        in_specs=[
            pl.BlockSpec((IN_PROJ_ROWS, D_MODEL), lambda i: (i, 0)),
            pl.BlockSpec((1, D_MODEL), lambda i: (0, 0)),
            pl.BlockSpec((D_MODEL, IN_W), lambda i: (0, 0),
                         pipeline_mode=pl.Buffered(1)),
        ],
        out_specs=pl.BlockSpec((IN_PROJ_ROWS, IN_W), lambda i: (i, 0)),
        out_shape=jax.ShapeDtypeStruct((n, IN_W), _BF16),
        scratch_shapes=[pltpu.VMEM((IN_PROJ_ROWS, D_MODEL), _BF16)],
        compiler_params=pltpu.CompilerParams(
            dimension_semantics=("arbitrary",), vmem_limit_bytes=VMEM_LIMIT),
        name="in_proj",
    )(x2, gain, w_bf)


def _attn_kernel(q_ref, kp_ref, km_ref, kn_ref, vp_ref, vm_ref, vn_ref, bias_ref,
                 o_ref, st_ref, kwin, vwin, *, n_steps):
    j = pl.program_id(2)
    kwin[0:N_SIDE] = kp_ref[...]
    kwin[N_SIDE:N_SIDE + Q_STEP] = km_ref[...]
    kwin[N_SIDE + Q_STEP:] = kn_ref[...]
    vwin[0:N_SIDE] = vp_ref[...]
    vwin[N_SIDE:N_SIDE + Q_STEP] = vm_ref[...]
    vwin[N_SIDE + Q_STEP:] = vn_ref[...]

    lane = lax.broadcasted_iota(jnp.int32, (Q_SUB, LANES), 1)
    low_head = lane < HEAD_DIM
    n_sub = Q_STEP // Q_SUB

    def sub_block(i, carry):
        r0 = pl.multiple_of(i * Q_SUB, Q_SUB)
        at_seq_start = jnp.logical_and(i == 0, j == 0)
        at_seq_end = jnp.logical_and(i == n_sub - 1, j == n_steps - 1)
        var = jnp.where(at_seq_start, 1, jnp.where(at_seq_end, 2, 0))
        stats = jnp.zeros((Q_SUB, LANES), _F32)
        for hp in range(HEADS_PER_GROUP // 2):
            cs = slice(hp * LANES, (hp + 1) * LANES)
            q2 = q_ref[pl.ds(r0, Q_SUB), cs] * (1.0 / math.sqrt(HEAD_DIM))
            k2 = kwin[pl.ds(r0, K_SUB), cs]
            v2 = vwin[pl.ds(r0, K_SUB), cs]
            outs = []
            for a in range(2):
                h = 2 * hp + a
                qm = jnp.where(low_head if a == 0 else jnp.logical_not(low_head),
                               q2, jnp.zeros_like(q2))
                s = lax.dot_general(qm, k2, (((1,), (1,)), ((), ())),
                                    preferred_element_type=_F32)
                s = s + bias_ref[var, h]
                m = jnp.max(s, axis=-1, keepdims=True)
                p = jnp.exp(s - m)
                l = jnp.sum(p, axis=-1, keepdims=True)
                o = jnp.dot(p.astype(_BF16), v2, preferred_element_type=_F32)
                outs.append(o * (1.0 / l))
                stats = jnp.where(lane == h, m + jnp.log(l), stats)
            o_ref[pl.ds(r0, Q_SUB), cs] = jnp.where(low_head, outs[0], outs[1]).astype(_BF16)
        st_ref[pl.ds(r0, Q_SUB), :] = stats
        return carry

    lax.fori_loop(0, n_sub, sub_block, 0)


def _attn_group(z3, bias, g, dil):
    b, s, _ = z3.shape
    sp = s // dil
    n_steps = sp // Q_STEP
    zr = z3.reshape(b, sp, dil * IN_W)
    cb = IN_W // BRANCH_W
    qc, kc, vc = (COL_Q // BRANCH_W + g, COL_K // BRANCH_W + g, COL_V // BRANCH_W + g)
    hb = Q_STEP // N_SIDE
    last_hb = sp // N_SIDE - 1

    def main(col):
        return pl.BlockSpec((None, Q_STEP, BRANCH_W), lambda bi, r, j: (bi, j, r * cb + col))

    def prev(col):
        return pl.BlockSpec((None, N_SIDE, BRANCH_W),
                            lambda bi, r, j: (bi, jnp.maximum(j * hb - 1, 0), r * cb + col))

    def nxt(col):
        return pl.BlockSpec((None, N_SIDE, BRANCH_W),
                            lambda bi, r, j: (bi, jnp.minimum((j + 1) * hb, last_hb), r * cb + col))

    o, st = pl.pallas_call(
        functools.partial(_attn_kernel, n_steps=n_steps),
        grid=(b, dil, n_steps),
        in_specs=[main(qc), prev(kc), main(kc), nxt(kc), prev(vc), main(vc), nxt(vc),
                  pl.BlockSpec(bias.shape, lambda bi, r, j: (0, 0, 0, 0))],
        out_specs=[
            pl.BlockSpec((None, Q_STEP, BRANCH_W), lambda bi, r, j: (bi, j, r)),
            pl.BlockSpec((None, Q_STEP, LANES), lambda bi, r, j: (bi, j, r)),
        ],
        out_shape=[
            jax.ShapeDtypeStruct((b, sp, dil * BRANCH_W), _BF16),
            jax.ShapeDtypeStruct((b, sp, dil * LANES), _F32),
        ],
        scratch_shapes=[pltpu.VMEM((Q_STEP + 2 * N_SIDE, BRANCH_W), _BF16),
                        pltpu.VMEM((Q_STEP + 2 * N_SIDE, BRANCH_W), _BF16)],
        compiler_params=pltpu.CompilerParams(
            dimension_semantics=("arbitrary", "arbitrary", "arbitrary"),
            vmem_limit_bytes=VMEM_LIMIT),
        name=f"attn_d{dil}",
    )(zr, zr, zr, zr, zr, zr, zr, bias)
    return o.reshape(b * s, BRANCH_W), st.reshape(b * s, LANES)


def _t5_bucket(rel):
    nb = NUM_BUCKETS // 2
    ret = (rel > 0).astype(jnp.int32) * nb
    n = jnp.abs(rel)
    max_exact = nb // 2
    nf = jnp.maximum(n, 1).astype(_F32)
    large = max_exact + (jnp.log(nf / max_exact) / math.log(MAX_DISTANCE / max_exact)
                         * (nb - max_exact)).astype(jnp.int32)
    large = jnp.minimum(large, nb - 1)
    return ret + jnp.where(n < max_exact, n, large)


def _band_bias(rel_bias, g, dil):
    offs = jnp.arange(-N_SIDE, N_SIDE + 1, dtype=jnp.int32) * dil
    per_off = rel_bias[_t5_bucket(offs)][:, g * HEADS_PER_GROUP:(g + 1) * HEADS_PER_GROUP]
    per_off = per_off.T.astype(_F32)
    qi = jnp.arange(Q_SUB, dtype=jnp.int32)[:, None]
    kj = jnp.arange(K_SUB, dtype=jnp.int32)[None, :]
    idx = kj - qi
    on_band = (idx >= 0) & (idx <= 2 * N_SIDE)
    band = jnp.where(on_band[None], per_off[:, jnp.clip(idx, 0, 2 * N_SIDE)], NEG_INF)
    first = jnp.where((kj < N_SIDE)[None], NEG_INF, band)
    last = jnp.where((kj >= K_SUB - N_SIDE)[None], NEG_INF, band)
    return jnp.stack([band, first, last], axis=0)


def _sigmoid(v):
    return 1.0 / (1.0 + jnp.exp(-v))


def _epilogue_kernel(x_ref, za_ref, zprev_ref, znext_ref, zbg_ref, zml_ref,
                     o1_ref, o2_ref, o3_ref, s1_ref, s2_ref, s3_ref,
                     wpool_ref, pscale_ref, wpa_ref, wpb_ref, wout_ref, bgate_ref,
                     fgain_ref, expand_ref, out_ref, ext_ref, *, seq_len):
    tm = x_ref.shape[0]
    i = pl.program_id(0)
    p0 = (i * tm) % seq_len
    has_prev = (p0 > 0).astype(_F32)
    has_next = (p0 + tm < seq_len).astype(_F32)

    a_in = za_ref[:, 0:BRANCH_W].astype(_F32)
    ext_ref[0:HALO_ROWS] = zprev_ref[...].astype(_F32) * has_prev
    ext_ref[HALO_ROWS:HALO_ROWS + tm] = a_in
    ext_ref[HALO_ROWS + tm:] = znext_ref[...].astype(_F32) * has_next
    pos = p0 + lax.broadcasted_iota(jnp.int32, (tm, 1), 0)
    mixed = []
    for gi, w in enumerate(POOL_WINDOWS):
        half = w // 2
        cs = slice(gi * POOL_GROUP_W, (gi + 1) * POOL_GROUP_W)
        win = ext_ref[pl.ds(HALO_ROWS - half, tm), cs]
        for o in range(-half + 1, half):
            win = win + ext_ref[pl.ds(HALO_ROWS + o, tm), cs]
        lo = jnp.maximum(pos - half, 0)
        hi = jnp.minimum(pos + half - 1, seq_len - 1)
        count = (hi - lo + 1).astype(_F32)
        pooled = win / count - a_in[:, cs]
        mixed.append(jnp.dot(pooled.astype(_BF16), wpool_ref[gi],
                             preferred_element_type=_F32))
    mixed = jnp.concatenate(mixed, axis=-1) * pscale_ref[...]
    a_gate = za_ref[:, BRANCH_W:].astype(_F32)
    ya_in = mixed * (a_gate * _sigmoid(a_gate))
    y_a = jnp.dot(ya_in.astype(_BF16), wpa_ref[...], preferred_element_type=_F32)

    l1, l2, l3 = s1_ref[...], s2_ref[...], s3_ref[...]
    mx = jnp.maximum(jnp.maximum(l1, l2), l3)
    e1, e2, e3 = jnp.exp(l1 - mx), jnp.exp(l2 - mx), jnp.exp(l3 - mx)
    inv = 1.0 / (e1 + e2 + e3)
    att = jnp.zeros((tm, BRANCH_W), _F32)
    for e, o_ref in ((e1, o1_ref), (e2, o2_ref), (e3, o3_ref)):
        wgt = e * inv
        hi_part = wgt.astype(_BF16)
        lo_part = (wgt - hi_part.astype(_F32)).astype(_BF16)
        wide = (jnp.dot(hi_part, expand_ref[...], preferred_element_type=_F32)
                + jnp.dot(lo_part, expand_ref[...], preferred_element_type=_F32))
        att = att + wide * o_ref[...].astype(_F32)
    b_gate = zbg_ref[...].astype(_F32)
    yb_in = att * (b_gate * _sigmoid(b_gate))
    y_b = jnp.dot(yb_in.astype(_BF16), wpb_ref[...], preferred_element_type=_F32)

    gate_a = _sigmoid(zml_ref[:, 0:D_MODEL].astype(_F32) + bgate_ref[0:1, :])
    gate_b = _sigmoid(zml_ref[:, D_MODEL:].astype(_F32) + bgate_ref[1:2, :])
    merged = gate_a * y_a + gate_b * y_b
    xo = x_ref[...] + jnp.dot(merged.astype(_BF16), wout_ref[...], preferred_element_type=_F32)
    r = lax.rsqrt(jnp.mean(xo * xo, axis=-1, keepdims=True) + RMS_EPS)
    out_ref[...] = (xo * r) * fgain_ref[...]


def _epilogue(x2, z, outs, stats, w_pool, pool_scale, w_pa, w_pb, w_out, b_gate,
              final_gain, seq_len):
    n = x2.shape[0]
    tm = EPI_ROWS
    halo_per_tile = tm // HALO_ROWS
    last_halo = n // HALO_ROWS - 1
    expand = (jnp.arange(LANES)[:, None] == jnp.arange(BRANCH_W)[None, :] // HEAD_DIM).astype(_BF16)

    def full(a):
        return pl.BlockSpec(a.shape, lambda i: (0,) * a.ndim)

    row = lambda w, col: pl.BlockSpec((tm, w), lambda i: (i, col))
    in_specs = [
        row(D_MODEL, 0),
        row(2 * BRANCH_W, 0),
        pl.BlockSpec((HALO_ROWS, BRANCH_W),
                     lambda i: (jnp.maximum(i * halo_per_tile - 1, 0), 0)),
        pl.BlockSpec((HALO_ROWS, BRANCH_W),
                     lambda i: (jnp.minimum((i + 1) * halo_per_tile, last_halo), 0)),
        row(BRANCH_W, COL_ATT_GATE // BRANCH_W),
        row(2 * D_MODEL, COL_MERGE // (2 * D_MODEL)),
        row(BRANCH_W, 0), row(BRANCH_W, 0), row(BRANCH_W, 0),
        row(LANES, 0), row(LANES, 0), row(LANES, 0),
        full(w_pool), full(pool_scale), full(w_pa), full(w_pb), full(w_out),
        full(b_gate), full(final_gain), full(expand),
    ]
    return pl.pallas_call(
        functools.partial(_epilogue_kernel, seq_len=seq_len),
        grid=(n // tm,),
        in_specs=in_specs,
        out_specs=pl.BlockSpec((tm, D_MODEL), lambda i: (i, 0)),
        out_shape=jax.ShapeDtypeStruct((n, D_MODEL), _F32),
        scratch_shapes=[pltpu.VMEM((tm + 2 * HALO_ROWS, BRANCH_W), _F32)],
        compiler_params=pltpu.CompilerParams(
            dimension_semantics=("arbitrary",), vmem_limit_bytes=VMEM_LIMIT),
        name="epilogue",
    )(x2, z, z, z, z, z, *outs, *stats, w_pool, pool_scale, w_pa, w_pb, w_out,
      b_gate, final_gain, expand)


def kernel(x, norm_gain, w_in, b_gate, rel_bias, w_pool, pool_scale, w_proj_a, w_proj_b,
           w_out, final_gain):
    b, s, d = x.shape
    assert d == D_MODEL and norm_gain.shape[0] == 1, "single-layer block of width D_MODEL"
    assert s % (Q_STEP * DILATED_GROUPS[-1][1]) == 0 and s % EPI_ROWS == 0
    x2 = x.reshape(b * s, d)
    z = _in_proj(x2, norm_gain[0][None, :], w_in[0].astype(_BF16))
    z3 = z.reshape(b, s, IN_W)
    outs, stats = [], []
    for g, (_, dil) in enumerate(DILATED_GROUPS):
        o, st = _attn_group(z3, _band_bias(rel_bias, g, dil), g, dil)
        outs.append(o)
        stats.append(st)
    y = _epilogue(x2, z, outs, stats, w_pool[0].astype(_BF16), pool_scale[0][None, :],
                  w_proj_a[0].astype(_BF16), w_proj_b[0].astype(_BF16),
                  w_out[0].astype(_BF16), b_gate[0], final_gain[None, :], s)
    return y.reshape(b, s, d)
```

```python
import functools
import math

import jax
import jax.numpy as jnp
from jax import lax
from jax.experimental import pallas as pl
from jax.experimental.pallas import tpu as pltpu

D_MODEL = 1024
BRANCH_W = D_MODEL // 2
POOL_WINDOWS = (2, 4, 8, 16)
POOL_GROUP_W = BRANCH_W // len(POOL_WINDOWS)
DILATED_GROUPS = ((128, 1), (512, 4), (2048, 16))
N_ATT_GROUPS = len(DILATED_GROUPS)
HEAD_DIM = 64
HEADS_PER_GROUP = BRANCH_W // HEAD_DIM
QKV_W = N_ATT_GROUPS * BRANCH_W
NUM_BUCKETS = 32
MAX_DISTANCE = 1024
IN_W = 2 * BRANCH_W + 3 * QKV_W + BRANCH_W + 2 * D_MODEL
RMS_EPS = 1e-6
NEG_INF = -1e30

COL_Q = 2 * BRANCH_W
COL_K = COL_Q + QKV_W
COL_V = COL_K + QKV_W
COL_ATT_GATE = COL_V + QKV_W
COL_MERGE = COL_ATT_GATE + BRANCH_W

NAT_ATT_GATE = 6
NAT_QKV = 7
NAT_W = 10 * BRANCH_W
QKV_G_W = 3 * BRANCH_W

N_SIDE = 64
LANES = 128
Q_SUB = 128
K_SUB = Q_SUB + 2 * N_SIDE
Q_STEP = 512
HALO_ROWS = 16

IN_PROJ_ROWS = 256
IN_PROJ_COLS = 512
EPI_ROWS = 256
VMEM_LIMIT = 56 * 1024 * 1024

_BF16 = jnp.bfloat16
_F32 = jnp.float32


def _in_proj_kernel(x_ref, g_ref, wn_ref, w4_ref, w16_ref, zn_ref, z4_ref, z16_ref,
                    hf_ref, hn_ref, h4_ref, h16_ref):
    tm = x_ref.shape[0]
    x = x_ref[...]
    r = lax.rsqrt(jnp.mean(x * x, axis=-1, keepdims=True) + RMS_EPS)
    hf = (x * r) * g_ref[...]
    n_slab = D_MODEL // LANES
    for c in range(n_slab):
        hf_ref[c] = hf[:, c * LANES:(c + 1) * LANES]
    hn_ref[...] = hf.astype(_BF16)
    for c in range(NAT_W // IN_PROJ_COLS):
        cs = slice(c * IN_PROJ_COLS, (c + 1) * IN_PROJ_COLS)
        zn_ref[:, cs] = jnp.dot(hn_ref[...], wn_ref[:, cs],
                                preferred_element_type=_F32).astype(_BF16)
    for dil, h_ref, w_ref, z_ref in ((4, h4_ref, w4_ref, z4_ref), (16, h16_ref, w16_ref, z16_ref)):
        rows = tm // dil
        for res in range(dil):
            for c in range(n_slab):
                h_ref[res * rows:(res + 1) * rows, c * LANES:(c + 1) * LANES] = (
                    hf_ref[c, pl.ds(res, rows, stride=dil), :].astype(_BF16))
        for c in range(QKV_G_W // IN_PROJ_COLS):
            cs = slice(c * IN_PROJ_COLS, (c + 1) * IN_PROJ_COLS)
            zc = jnp.dot(h_ref[...], w_ref[:, cs], preferred_element_type=_F32).astype(_BF16)
            for res in range(dil):
                z_ref[res, :, cs] = zc[res * rows:(res + 1) * rows]


def _in_proj(x, gain, w_nat, w_d4, w_d16):
    b, s, _ = x.shape
    tm = IN_PROJ_ROWS
    resident = lambda a: pl.BlockSpec(a.shape, lambda bi, i: (0, 0), pipeline_mode=pl.Buffered(1))
    return pl.pallas_call(
        _in_proj_kernel,
        grid=(b, s // tm),
        in_specs=[
            pl.BlockSpec((None, tm, D_MODEL), lambda bi, i: (bi, i, 0)),
            pl.BlockSpec((1, D_MODEL), lambda bi, i: (0, 0)),
            resident(w_nat), resident(w_d4), resident(w_d16),
        ],
        out_specs=[
            pl.BlockSpec((None, tm, NAT_W), lambda bi, i: (bi, i, 0)),
            pl.BlockSpec((None, 4, tm // 4, QKV_G_W), lambda bi, i: (bi, 0, i, 0)),
            pl.BlockSpec((None, 16, tm // 16, QKV_G_W), lambda bi, i: (bi, 0, i, 0)),
        ],
        out_shape=[
            jax.ShapeDtypeStruct((b, s, NAT_W), _BF16),
            jax.ShapeDtypeStruct((b, 4, s // 4, QKV_G_W), _BF16),
            jax.ShapeDtypeStruct((b, 16, s // 16, QKV_G_W), _BF16),
        ],
        scratch_shapes=[pltpu.VMEM((D_MODEL // LANES, tm, LANES), _F32),
                        pltpu.VMEM((tm, D_MODEL), _BF16),
                        pltpu.VMEM((tm, D_MODEL), _BF16), pltpu.VMEM((tm, D_MODEL), _BF16)],
        compiler_params=pltpu.CompilerParams(
            dimension_semantics=("arbitrary", "arbitrary"), vmem_limit_bytes=VMEM_LIMIT),
        name="in_proj",
    )(x, gain, w_nat, w_d4, w_d16)


def _attn_kernel(q_ref, kp_ref, km_ref, kn_ref, vp_ref, vm_ref, vn_ref, bias_ref,
                 o_ref, st_ref, kwin, vwin, *, n_steps):
    j = pl.program_id(2)
    kwin[0:N_SIDE] = kp_ref[...]
    kwin[N_SIDE:N_SIDE + Q_STEP] = km_ref[...]
    kwin[N_SIDE + Q_STEP:] = kn_ref[...]
    vwin[0:N_SIDE] = vp_ref[...]
    vwin[N_SIDE:N_SIDE + Q_STEP] = vm_ref[...]
    vwin[N_SIDE + Q_STEP:] = vn_ref[...]

    lane = lax.broadcasted_iota(jnp.int32, (Q_SUB, LANES), 1)
    low_head = lane < HEAD_DIM
    n_sub = Q_STEP // Q_SUB

    def sub_block(i, carry):
        r0 = pl.multiple_of(i * Q_SUB, Q_SUB)
        at_seq_start = jnp.logical_and(i == 0, j == 0)
        at_seq_end = jnp.logical_and(i == n_sub - 1, j == n_steps - 1)
        var = jnp.where(at_seq_start, 1, jnp.where(at_seq_end, 2, 0))
        stats = jnp.zeros((Q_SUB, LANES), _F32)
        for hp in range(HEADS_PER_GROUP // 2):
            cs = slice(hp * LANES, (hp + 1) * LANES)
            q2 = q_ref[pl.ds(r0, Q_SUB), cs] * (1.0 / math.sqrt(HEAD_DIM))
            k2 = kwin[pl.ds(r0, K_SUB), cs]
            v2 = vwin[pl.ds(r0, K_SUB), cs]
            outs = []
            for a in range(2):
                h = 2 * hp + a
                qm = jnp.where(low_head if a == 0 else jnp.logical_not(low_head),
                               q2, jnp.zeros_like(q2))
                s = lax.dot_general(qm, k2, (((1,), (1,)), ((), ())),
                                    preferred_element_type=_F32)
                s = s + bias_ref[var, h]
                m = jnp.max(s, axis=-1, keepdims=True)
                p = jnp.exp(s - m)
                l = jnp.sum(p, axis=-1, keepdims=True)
                o = jnp.dot(p.astype(_BF16), v2, preferred_element_type=_F32)
                outs.append(o * (1.0 / l))
                stats = jnp.where(lane == h, m + jnp.log(l), stats)
            o_ref[pl.ds(r0, Q_SUB), cs] = jnp.where(low_head, outs[0], outs[1]).astype(_BF16)
        st_ref[pl.ds(r0, Q_SUB), :] = stats
        return carry

    lax.fori_loop(0, n_sub, sub_block, 0)


def _attn_group(zg, bias, col0):
    b, dil, sp, _ = zg.shape
    n_steps = sp // Q_STEP
    hb = Q_STEP // N_SIDE
    last_hb = sp // N_SIDE - 1

    def main(col):
        return pl.BlockSpec((None, None, Q_STEP, BRANCH_W), lambda bi, r, j: (bi, r, j, col))

    def prev(col):
        return pl.BlockSpec((None, None, N_SIDE, BRANCH_W),
                            lambda bi, r, j: (bi, r, jnp.maximum(j * hb - 1, 0), col))

    def nxt(col):
        return pl.BlockSpec((None, None, N_SIDE, BRANCH_W),
                            lambda bi, r, j: (bi, r, jnp.minimum((j + 1) * hb, last_hb), col))

    qc, kc, vc = col0, col0 + 1, col0 + 2
    return pl.pallas_call(
        functools.partial(_attn_kernel, n_steps=n_steps),
        grid=(b, dil, n_steps),
        in_specs=[main(qc), prev(kc), main(kc), nxt(kc), prev(vc), main(vc), nxt(vc),
                  pl.BlockSpec(bias.shape, lambda bi, r, j: (0, 0, 0, 0))],
        out_specs=[
            pl.BlockSpec((None, None, Q_STEP, BRANCH_W), lambda bi, r, j: (bi, r, j, 0)),
            pl.BlockSpec((None, None, Q_STEP, LANES), lambda bi, r, j: (bi, r, j, 0)),
        ],
        out_shape=[
            jax.ShapeDtypeStruct((b, dil, sp, BRANCH_W), _BF16),
            jax.ShapeDtypeStruct((b, dil, sp, LANES), _F32),
        ],
        scratch_shapes=[pltpu.VMEM((Q_STEP + 2 * N_SIDE, BRANCH_W), _BF16),
                        pltpu.VMEM((Q_STEP + 2 * N_SIDE, BRANCH_W), _BF16)],
        compiler_params=pltpu.CompilerParams(
            dimension_semantics=("arbitrary", "arbitrary", "arbitrary"),
            vmem_limit_bytes=VMEM_LIMIT),
        name=f"attn_d{dil}",
    )(zg, zg, zg, zg, zg, zg, zg, bias)


def _t5_bucket(rel):
    nb = NUM_BUCKETS // 2
    ret = (rel > 0).astype(jnp.int32) * nb
    n = jnp.abs(rel)
    max_exact = nb // 2
    nf = jnp.maximum(n, 1).astype(_F32)
    large = max_exact + (jnp.log(nf / max_exact) / math.log(MAX_DISTANCE / max_exact)
                         * (nb - max_exact)).astype(jnp.int32)
    large = jnp.minimum(large, nb - 1)
    return ret + jnp.where(n < max_exact, n, large)


def _band_bias(rel_bias, g, dil):
    n_off = 2 * N_SIDE + 1
    offs = jnp.arange(-N_SIDE, N_SIDE + 1, dtype=jnp.int32) * dil
    per_off = rel_bias[_t5_bucket(offs)][:, g * HEADS_PER_GROUP:(g + 1) * HEADS_PER_GROUP]
    per_off = per_off.T.astype(_F32)
    ext = jnp.concatenate(
        [per_off, jnp.full((HEADS_PER_GROUP, K_SUB + 1 - n_off), NEG_INF, _F32)], axis=1)
    band = jnp.tile(ext, (1, Q_SUB))[:, :Q_SUB * K_SUB].reshape(HEADS_PER_GROUP, Q_SUB, K_SUB)
    kj = jnp.arange(K_SUB, dtype=jnp.int32)[None, None, :]
    first = jnp.where(kj < N_SIDE, NEG_INF, band)
    last = jnp.where(kj >= K_SUB - N_SIDE, NEG_INF, band)
    return jnp.stack([band, first, last], axis=0)


def _sigmoid(v):
    return 1.0 / (1.0 + jnp.exp(-v))


def _epilogue_kernel(x_ref, za_ref, zprev_ref, znext_ref, zga_ref, zgb_ref, zbg_ref,
                     o1_ref, o4_ref, o16_ref, s1_ref, s4_ref, s16_ref,
                     wpool_ref, pscale_ref, wpa_ref, wpb_ref, wout_ref, bgate_ref,
                     fgain_ref, expand_ref, out_ref,
                     ext_ref, o4n_ref, o16n_ref, s4n_ref, s16n_ref, *, seq_len):
    tm = x_ref.shape[0]
    p0 = pl.program_id(1) * tm
    has_prev = (p0 > 0).astype(_F32)
    has_next = (p0 + tm < seq_len).astype(_F32)

    a_in = za_ref[:, 0:BRANCH_W].astype(_F32)
    ext_ref[0:HALO_ROWS] = zprev_ref[...].astype(_F32) * has_prev
    ext_ref[HALO_ROWS:HALO_ROWS + tm] = a_in
    ext_ref[HALO_ROWS + tm:] = znext_ref[...].astype(_F32) * has_next
    pos = p0 + lax.broadcasted_iota(jnp.int32, (tm, 1), 0)
    mixed = []
    for gi, w in enumerate(POOL_WINDOWS):
        half = w // 2
        cs = slice(gi * POOL_GROUP_W, (gi + 1) * POOL_GROUP_W)
        win = ext_ref[pl.ds(HALO_ROWS - half, tm), cs]
        for o in range(-half + 1, half):
            win = win + ext_ref[pl.ds(HALO_ROWS + o, tm), cs]
        lo = jnp.maximum(pos - half, 0)
        hi = jnp.minimum(pos + half - 1, seq_len - 1)
        count = (hi - lo + 1).astype(_F32)
        pooled = win / count - a_in[:, cs]
        mixed.append(jnp.dot(pooled.astype(_BF16), wpool_ref[gi],
                             preferred_element_type=_F32))
    mixed = jnp.concatenate(mixed, axis=-1) * pscale_ref[...]
    a_gate = za_ref[:, BRANCH_W:].astype(_F32)
    ya_in = mixed * (a_gate * _sigmoid(a_gate))
    y_a = jnp.dot(ya_in.astype(_BF16), wpa_ref[...], preferred_element_type=_F32)

    for dil, o_ref, on_ref, s_ref, sn_ref in ((4, o4_ref, o4n_ref, s4_ref, s4n_ref),
                                              (16, o16_ref, o16n_ref, s16_ref, s16n_ref)):
        rows = tm // dil
        for res in range(dil):
            o_res = o_ref[res].astype(_F32)
            for c in range(BRANCH_W // LANES):
                on_ref[c, pl.ds(res, rows, stride=dil), :] = o_res[:, c * LANES:(c + 1) * LANES]
            sn_ref[pl.ds(res, rows, stride=dil), :] = s_ref[res]
    l1, l2, l3 = s1_ref[...], s4n_ref[...], s16n_ref[...]
    mx = jnp.maximum(jnp.maximum(l1, l2), l3)
    e1, e2, e3 = jnp.exp(l1 - mx), jnp.exp(l2 - mx), jnp.exp(l3 - mx)
    inv = 1.0 / (e1 + e2 + e3)
    att = jnp.zeros((tm, BRANCH_W), _F32)
    slabs = lambda ref: jnp.concatenate([ref[c] for c in range(BRANCH_W // LANES)], axis=-1)
    for e, og in ((e1, o1_ref[...].astype(_F32)), (e2, slabs(o4n_ref)), (e3, slabs(o16n_ref))):
        wgt = e * inv
        hi_part = wgt.astype(_BF16)
        lo_part = (wgt - hi_part.astype(_F32)).astype(_BF16)
        wide = (jnp.dot(hi_part, expand_ref[...], preferred_element_type=_F32)
                + jnp.dot(lo_part, expand_ref[...], preferred_element_type=_F32))
        att = att + wide * og
    b_gate = zbg_ref[...].astype(_F32)
    yb_in = att * (b_gate * _sigmoid(b_gate))
    y_b = jnp.dot(yb_in.astype(_BF16), wpb_ref[...], preferred_element_type=_F32)

    gate_a = _sigmoid(zga_ref[...].astype(_F32) + bgate_ref[0:1, :])
    gate_b = _sigmoid(zgb_ref[...].astype(_F32) + bgate_ref[1:2, :])
    merged = gate_a * y_a + gate_b * y_b
    xo = x_ref[...] + jnp.dot(merged.astype(_BF16), wout_ref[...], preferred_element_type=_F32)
    r = lax.rsqrt(jnp.mean(xo * xo, axis=-1, keepdims=True) + RMS_EPS)
    out_ref[...] = (xo * r) * fgain_ref[...]


def _epilogue(x, z_nat, outs, stats, w_pool, pool_scale, w_pa, w_pb, w_out, b_gate, final_gain):
    b, s, _ = x.shape
    tm = EPI_ROWS
    halo_per_tile = tm // HALO_ROWS
    last_halo = s // HALO_ROWS - 1
    expand = (jnp.arange(LANES)[:, None] == jnp.arange(BRANCH_W)[None, :] // HEAD_DIM).astype(_BF16)

    def full(a):
        return pl.BlockSpec(a.shape, lambda bi, i: (0,) * a.ndim)

    def row(w, col):
        return pl.BlockSpec((None, tm, w), lambda bi, i: (bi, i, col))

    def by_residue(a):
        dil, w = a.shape[1], a.shape[3]
        return pl.BlockSpec((None, dil, tm // dil, w), lambda bi, i: (bi, 0, i, 0))

    in_specs = [
        row(D_MODEL, 0),
        row(2 * BRANCH_W, 0),
        pl.BlockSpec((None, HALO_ROWS, BRANCH_W),
                     lambda bi, i: (bi, jnp.maximum(i * halo_per_tile - 1, 0), 0)),
        pl.BlockSpec((None, HALO_ROWS, BRANCH_W),
                     lambda bi, i: (bi, jnp.minimum((i + 1) * halo_per_tile, last_halo), 0)),
        row(D_MODEL, 1), row(D_MODEL, 2),
        row(BRANCH_W, NAT_ATT_GATE),
        pl.BlockSpec((None, None, tm, BRANCH_W), lambda bi, i: (bi, 0, i, 0)),
        by_residue(outs[1]), by_residue(outs[2]),
        pl.BlockSpec((None, None, tm, LANES), lambda bi, i: (bi, 0, i, 0)),
        by_residue(stats[1]), by_residue(stats[2]),
        full(w_pool), full(pool_scale), full(w_pa), full(w_pb), full(w_out),
        full(b_gate), full(final_gain), full(expand),
    ]
    return pl.pallas_call(
        functools.partial(_epilogue_kernel, seq_len=s),
        grid=(b, s // tm),
        in_specs=in_specs,
        out_specs=pl.BlockSpec((None, tm, D_MODEL), lambda bi, i: (bi, i, 0)),
        out_shape=jax.ShapeDtypeStruct((b, s, D_MODEL), _F32),
        scratch_shapes=[pltpu.VMEM((tm + 2 * HALO_ROWS, BRANCH_W), _F32),
                        pltpu.VMEM((BRANCH_W // LANES, tm, LANES), _F32),
                        pltpu.VMEM((BRANCH_W // LANES, tm, LANES), _F32),
                        pltpu.VMEM((tm, LANES), _F32), pltpu.VMEM((tm, LANES), _F32)],
        compiler_params=pltpu.CompilerParams(
            dimension_semantics=("arbitrary", "arbitrary"), vmem_limit_bytes=VMEM_LIMIT),
        name="epilogue",
    )(x, z_nat, z_nat, z_nat, z_nat, z_nat, z_nat, *outs, *stats,
      w_pool, pool_scale, w_pa, w_pb, w_out, b_gate, final_gain, expand)


def _split_w_in(w):
    def qkv(g):
        return [w[:, c + g * BRANCH_W:c + (g + 1) * BRANCH_W] for c in (COL_Q, COL_K, COL_V)]
    nat = [w[:, :COL_Q], w[:, COL_MERGE:], w[:, COL_ATT_GATE:COL_MERGE]] + qkv(0)
    cat = lambda parts: jnp.concatenate(parts, axis=1).astype(_BF16)
    return cat(nat), cat(qkv(1)), cat(qkv(2))


def kernel(x, norm_gain, w_in, b_gate, rel_bias, w_pool, pool_scale, w_proj_a, w_proj_b,
           w_out, final_gain):
    b, s, d = x.shape
    assert d == D_MODEL and norm_gain.shape[0] == 1, "single-layer block of width D_MODEL"
    assert s % (Q_STEP * DILATED_GROUPS[-1][1]) == 0 and s % EPI_ROWS == 0
    assert [dil for _, dil in DILATED_GROUPS] == [1, 4, 16]
    w_nat, w_d4, w_d16 = _split_w_in(w_in[0])
    z_nat, z_d4, z_d16 = _in_proj(x, norm_gain[0][None, :], w_nat, w_d4, w_d16)
    outs, stats = [], []
    for g, (zg, col0) in enumerate(((z_nat[:, None], NAT_QKV), (z_d4, 0), (z_d16, 0))):
        o, st = _attn_group(zg, _band_bias(rel_bias, g, DILATED_GROUPS[g][1]), col0)
        outs.append(o)
        stats.append(st)
    return _epilogue(x, z_nat, outs, stats, w_pool[0].astype(_BF16), pool_scale[0][None, :],
                     w_proj_a[0].astype(_BF16), w_proj_b[0].astype(_BF16),
                     w_out[0].astype(_BF16), b_gate[0], final_gain[None, :])
```

```python
import functools
import math

import jax
import jax.numpy as jnp
from jax import lax
from jax.experimental import pallas as pl
from jax.experimental.pallas import tpu as pltpu

D_MODEL = 1024
BRANCH_W = D_MODEL // 2
POOL_WINDOWS = (2, 4, 8, 16)
POOL_GROUP_W = BRANCH_W // len(POOL_WINDOWS)
DILATED_GROUPS = ((128, 1), (512, 4), (2048, 16))
N_ATT_GROUPS = len(DILATED_GROUPS)
HEAD_DIM = 64
HEADS_PER_GROUP = BRANCH_W // HEAD_DIM
QKV_W = N_ATT_GROUPS * BRANCH_W
NUM_BUCKETS = 32
MAX_DISTANCE = 1024
IN_W = 2 * BRANCH_W + 3 * QKV_W + BRANCH_W + 2 * D_MODEL
RMS_EPS = 1e-6
NEG_INF = -1e30
LOG2_E = math.log2(math.e)
SCORE_SCALE = LOG2_E / math.sqrt(HEAD_DIM)

COL_Q = 2 * BRANCH_W
COL_K = COL_Q + QKV_W
COL_V = COL_K + QKV_W
COL_ATT_GATE = COL_V + QKV_W
COL_MERGE = COL_ATT_GATE + BRANCH_W

NAT_ATT_GATE = 6
NAT_QKV = 7
NAT_W = 10 * BRANCH_W
QKV_G_W = 3 * BRANCH_W

N_SIDE = 64
LANES = 128
Q_SUB = 128
K_SUB = Q_SUB + 2 * N_SIDE
Q_STEP = 512
HALO_ROWS = 16

IN_PROJ_ROWS = 256
IN_PROJ_COLS = 512
EPI_ROWS = 256
VMEM_LIMIT = 56 * 1024 * 1024

_BF16 = jnp.bfloat16
_F32 = jnp.float32


def _in_proj_kernel(x_ref, g_ref, wn_ref, w4_ref, w16_ref, zn_ref, z4_ref, z16_ref,
                    hf_ref, hn_ref, h4_ref, h16_ref):
    tm = x_ref.shape[0]
    x = x_ref[...]
    r = lax.rsqrt(jnp.mean(x * x, axis=-1, keepdims=True) + RMS_EPS)
    hf = (x * r) * g_ref[...]
    n_slab = D_MODEL // LANES
    for c in range(n_slab):
        hf_ref[c] = hf[:, c * LANES:(c + 1) * LANES]
    hn_ref[...] = hf.astype(_BF16)
    for c in range(NAT_W // IN_PROJ_COLS):
        cs = slice(c * IN_PROJ_COLS, (c + 1) * IN_PROJ_COLS)
        zn_ref[:, cs] = jnp.dot(hn_ref[...], wn_ref[:, cs],
                                preferred_element_type=_F32).astype(_BF16)
    for dil, h_ref, w_ref, z_ref in ((4, h4_ref, w4_ref, z4_ref), (16, h16_ref, w16_ref, z16_ref)):
        rows = tm // dil
        for res in range(dil):
            for c in range(n_slab):
                h_ref[res * rows:(res + 1) * rows, c * LANES:(c + 1) * LANES] = (
                    hf_ref[c, pl.ds(res, rows, stride=dil), :].astype(_BF16))
        for c in range(QKV_G_W // IN_PROJ_COLS):
            cs = slice(c * IN_PROJ_COLS, (c + 1) * IN_PROJ_COLS)
            zc = jnp.dot(h_ref[...], w_ref[:, cs], preferred_element_type=_F32).astype(_BF16)
            for res in range(dil):
                z_ref[res, :, cs] = zc[res * rows:(res + 1) * rows]


def _in_proj(x, gain, w_nat, w_d4, w_d16):
    b, s, _ = x.shape
    tm = IN_PROJ_ROWS
    resident = lambda a: pl.BlockSpec(a.shape, lambda bi, i: (0, 0), pipeline_mode=pl.Buffered(1))
    return pl.pallas_call(
        _in_proj_kernel,
        grid=(b, s // tm),
        in_specs=[
            pl.BlockSpec((None, tm, D_MODEL), lambda bi, i: (bi, i, 0)),
            pl.BlockSpec((1, D_MODEL), lambda bi, i: (0, 0)),
            resident(w_nat), resident(w_d4), resident(w_d16),
        ],
        out_specs=[
            pl.BlockSpec((None, tm, NAT_W), lambda bi, i: (bi, i, 0)),
            pl.BlockSpec((None, 4, tm // 4, QKV_G_W), lambda bi, i: (bi, 0, i, 0)),
            pl.BlockSpec((None, 16, tm // 16, QKV_G_W), lambda bi, i: (bi, 0, i, 0)),
        ],
        out_shape=[
            jax.ShapeDtypeStruct((b, s, NAT_W), _BF16),
            jax.ShapeDtypeStruct((b, 4, s // 4, QKV_G_W), _BF16),
            jax.ShapeDtypeStruct((b, 16, s // 16, QKV_G_W), _BF16),
        ],
        scratch_shapes=[pltpu.VMEM((D_MODEL // LANES, tm, LANES), _F32),
                        pltpu.VMEM((tm, D_MODEL), _BF16),
                        pltpu.VMEM((tm, D_MODEL), _BF16), pltpu.VMEM((tm, D_MODEL), _BF16)],
        compiler_params=pltpu.CompilerParams(
            dimension_semantics=("arbitrary", "arbitrary"), vmem_limit_bytes=VMEM_LIMIT),
        name="in_proj",
    )(x, gain, w_nat, w_d4, w_d16)


def _attn_kernel(q_ref, kp_ref, km_ref, kn_ref, vp_ref, vm_ref, vn_ref, bias_ref,
                 o_ref, st_ref, kwin, vwin, *, n_steps):
    j = pl.program_id(2)
    kwin[0:N_SIDE] = kp_ref[...]
    kwin[N_SIDE:N_SIDE + Q_STEP] = km_ref[...]
    kwin[N_SIDE + Q_STEP:] = kn_ref[...]
    vwin[0:N_SIDE] = vp_ref[...]
    vwin[N_SIDE:N_SIDE + Q_STEP] = vm_ref[...]
    vwin[N_SIDE + Q_STEP:] = vn_ref[...]

    lane = lax.broadcasted_iota(jnp.int32, (Q_SUB, LANES), 1)
    low_head = lane < HEAD_DIM
    n_sub = Q_STEP // Q_SUB

    def sub_block(i, carry):
        r0 = pl.multiple_of(i * Q_SUB, Q_SUB)
        at_seq_start = jnp.logical_and(i == 0, j == 0)
        at_seq_end = jnp.logical_and(i == n_sub - 1, j == n_steps - 1)
        var = jnp.where(at_seq_start, 1, jnp.where(at_seq_end, 2, 0))
        stats = jnp.zeros((Q_SUB, LANES), _F32)
        for hp in range(HEADS_PER_GROUP // 2):
            cs = slice(hp * LANES, (hp + 1) * LANES)
            q2 = q_ref[pl.ds(r0, Q_SUB), cs]
            k2 = kwin[pl.ds(r0, K_SUB), cs]
            v2 = jnp.concatenate([vwin[pl.ds(r0, K_SUB), cs],
                                  jnp.ones((K_SUB, LANES), _BF16)], axis=1)
            zero = jnp.zeros_like(q2)
            qs = jnp.concatenate([jnp.where(low_head, q2, zero),
                                  jnp.where(low_head, zero, q2)], axis=0)
            s = lax.dot_general(qs, k2, (((1,), (1,)), ((), ())),
                                preferred_element_type=_F32)
            s = s + bias_ref[var, hp]
            m = jnp.max(s, axis=-1, keepdims=True)
            p = jnp.exp2(s - m)
            ol = jnp.dot(p.astype(_BF16), v2, preferred_element_type=_F32)
            o, l = ol[:, :LANES], ol[:, LANES:]
            o_ref[pl.ds(r0, Q_SUB), cs] = jnp.where(low_head, o[:Q_SUB], o[Q_SUB:]).astype(_BF16)
            for a in range(2):
                h = 2 * hp + a
                rows = slice(a * Q_SUB, (a + 1) * Q_SUB)
                stats = jnp.where(lane == h, m[rows], stats)
                stats = jnp.where(lane == HEADS_PER_GROUP + h, l[rows], stats)
        st_ref[pl.ds(r0, Q_SUB), :] = stats
        return carry

    lax.fori_loop(0, n_sub, sub_block, 0, unroll=True)


def _attn_group(zg, bias, col0):
    b, dil, sp, _ = zg.shape
    n_steps = sp // Q_STEP
    hb = Q_STEP // N_SIDE
    last_hb = sp // N_SIDE - 1

    def main(col):
        return pl.BlockSpec((None, None, Q_STEP, BRANCH_W), lambda bi, r, j: (bi, r, j, col))

    def prev(col):
        return pl.BlockSpec((None, None, N_SIDE, BRANCH_W),
                            lambda bi, r, j: (bi, r, jnp.maximum(j * hb - 1, 0), col))

    def nxt(col):
        return pl.BlockSpec((None, None, N_SIDE, BRANCH_W),
                            lambda bi, r, j: (bi, r, jnp.minimum((j + 1) * hb, last_hb), col))

    qc, kc, vc = col0, col0 + 1, col0 + 2
    return pl.pallas_call(
        functools.partial(_attn_kernel, n_steps=n_steps),
        grid=(b, dil, n_steps),
        in_specs=[main(qc), prev(kc), main(kc), nxt(kc), prev(vc), main(vc), nxt(vc),
                  pl.BlockSpec(bias.shape, lambda bi, r, j: (0, 0, 0, 0))],
        out_specs=[
            pl.BlockSpec((None, None, Q_STEP, BRANCH_W), lambda bi, r, j: (bi, r, j, 0)),
            pl.BlockSpec((None, None, Q_STEP, LANES), lambda bi, r, j: (bi, r, j, 0)),
        ],
        out_shape=[
            jax.ShapeDtypeStruct((b, dil, sp, BRANCH_W), _BF16),
            jax.ShapeDtypeStruct((b, dil, sp, LANES), _F32),
        ],
        scratch_shapes=[pltpu.VMEM((Q_STEP + 2 * N_SIDE, BRANCH_W), _BF16),
                        pltpu.VMEM((Q_STEP + 2 * N_SIDE, BRANCH_W), _BF16)],
        compiler_params=pltpu.CompilerParams(
            dimension_semantics=("arbitrary", "arbitrary", "arbitrary"),
            vmem_limit_bytes=VMEM_LIMIT),
        name=f"attn_d{dil}",
    )(zg, zg, zg, zg, zg, zg, zg, bias)


def _t5_bucket(rel):
    nb = NUM_BUCKETS // 2
    ret = (rel > 0).astype(jnp.int32) * nb
    n = jnp.abs(rel)
    max_exact = nb // 2
    nf = jnp.maximum(n, 1).astype(_F32)
    large = max_exact + (jnp.log(nf / max_exact) / math.log(MAX_DISTANCE / max_exact)
                         * (nb - max_exact)).astype(jnp.int32)
    large = jnp.minimum(large, nb - 1)
    return ret + jnp.where(n < max_exact, n, large)


def _band_bias(rel_bias, g, dil):
    n_off = 2 * N_SIDE + 1
    offs = jnp.arange(-N_SIDE, N_SIDE + 1, dtype=jnp.int32) * dil
    per_off = rel_bias[_t5_bucket(offs)][:, g * HEADS_PER_GROUP:(g + 1) * HEADS_PER_GROUP]
    per_off = per_off.T.astype(_F32) * LOG2_E
    ext = jnp.concatenate(
        [per_off, jnp.full((HEADS_PER_GROUP, K_SUB + 1 - n_off), NEG_INF, _F32)], axis=1)
    band = jnp.tile(ext, (1, Q_SUB))[:, :Q_SUB * K_SUB].reshape(HEADS_PER_GROUP, Q_SUB, K_SUB)
    kj = jnp.arange(K_SUB, dtype=jnp.int32)[None, None, :]
    first = jnp.where(kj < N_SIDE, NEG_INF, band)
    last = jnp.where(kj >= K_SUB - N_SIDE, NEG_INF, band)
    return jnp.stack([band, first, last], axis=0).reshape(
        3, HEADS_PER_GROUP // 2, 2 * Q_SUB, K_SUB)


def _sigmoid(v):
    return 1.0 / (1.0 + jnp.exp(-v))


def _epilogue_kernel(x_ref, za_ref, zprev_ref, znext_ref, zga_ref, zgb_ref, zbg_ref,
                     o1_ref, o4_ref, o16_ref, s1_ref, s4_ref, s16_ref,
                     wpool_ref, pscale_ref, wpa_ref, wpb_ref, wout_ref, bgate_ref,
                     fgain_ref, expand_ref, out_ref,
                     ext_ref, o4n_ref, o16n_ref, s4n_ref, s16n_ref, *, seq_len):
    tm = x_ref.shape[0]
    p0 = pl.program_id(1) * tm
    has_prev = (p0 > 0).astype(_F32)
    has_next = (p0 + tm < seq_len).astype(_F32)

    a_in = za_ref[:, 0:BRANCH_W].astype(_F32)
    ext_ref[0:HALO_ROWS] = zprev_ref[...].astype(_F32) * has_prev
    ext_ref[HALO_ROWS:HALO_ROWS + tm] = a_in
    ext_ref[HALO_ROWS + tm:] = znext_ref[...].astype(_F32) * has_next
    pos = p0 + lax.broadcasted_iota(jnp.int32, (tm, 1), 0)
    mixed = []
    for gi, w in enumerate(POOL_WINDOWS):
        half = w // 2
        cs = slice(gi * POOL_GROUP_W, (gi + 1) * POOL_GROUP_W)
        win = ext_ref[pl.ds(HALO_ROWS - half, tm), cs]
        for o in range(-half + 1, half):
            win = win + ext_ref[pl.ds(HALO_ROWS + o, tm), cs]
        lo = jnp.maximum(pos - half, 0)
        hi = jnp.minimum(pos + half - 1, seq_len - 1)
        count = (hi - lo + 1).astype(_F32)
        pooled = win / count - a_in[:, cs]
        mixed.append(jnp.dot(pooled.astype(_BF16), wpool_ref[gi],
                             preferred_element_type=_F32))
    mixed = jnp.concatenate(mixed, axis=-1) * pscale_ref[...]
    a_gate = za_ref[:, BRANCH_W:].astype(_F32)
    ya_in = mixed * (a_gate * _sigmoid(a_gate))
    y_a = jnp.dot(ya_in.astype(_BF16), wpa_ref[...], preferred_element_type=_F32)

    for dil, o_ref, on_ref, s_ref, sn_ref in ((4, o4_ref, o4n_ref, s4_ref, s4n_ref),
                                              (16, o16_ref, o16n_ref, s16_ref, s16n_ref)):
        rows = tm // dil
        for res in range(dil):
            o_res = o_ref[res].astype(_F32)
            for c in range(BRANCH_W // LANES):
                on_ref[c, pl.ds(res, rows, stride=dil), :] = o_res[:, c * LANES:(c + 1) * LANES]
            sn_ref[pl.ds(res, rows, stride=dil), :] = s_ref[res]
    st = (s1_ref[...], s4n_ref[...], s16n_ref[...])
    mx = jnp.maximum(jnp.maximum(st[0], st[1]), st[2])
    es = [jnp.exp2(t - mx) for t in st]
    ls = [pltpu.roll(t, LANES - HEADS_PER_GROUP, 1) for t in st]
    head_lane = lax.broadcasted_iota(jnp.int32, (tm, LANES), 1) < HEADS_PER_GROUP
    den = jnp.where(head_lane, es[0] * ls[0] + es[1] * ls[1] + es[2] * ls[2], 1.0)
    inv = 1.0 / den
    att = jnp.zeros((tm, BRANCH_W), _F32)
    slabs = lambda ref: jnp.concatenate([ref[c] for c in range(BRANCH_W // LANES)], axis=-1)
    for e, og in zip(es, (o1_ref[...].astype(_F32), slabs(o4n_ref), slabs(o16n_ref))):
        wgt = jnp.where(head_lane, e * inv, 0.0)
        hi_part = wgt.astype(_BF16)
        lo_part = (wgt - hi_part.astype(_F32)).astype(_BF16)
        wide = (jnp.dot(hi_part, expand_ref[...], preferred_element_type=_F32)
                + jnp.dot(lo_part, expand_ref[...], preferred_element_type=_F32))
        att = att + wide * og
    b_gate = zbg_ref[...].astype(_F32)
    yb_in = att * (b_gate * _sigmoid(b_gate))
    y_b = jnp.dot(yb_in.astype(_BF16), wpb_ref[...], preferred_element_type=_F32)

    gate_a = _sigmoid(zga_ref[...].astype(_F32) + bgate_ref[0:1, :])
    gate_b = _sigmoid(zgb_ref[...].astype(_F32) + bgate_ref[1:2, :])
    merged = gate_a * y_a + gate_b * y_b
    xo = x_ref[...] + jnp.dot(merged.astype(_BF16), wout_ref[...], preferred_element_type=_F32)
    r = lax.rsqrt(jnp.mean(xo * xo, axis=-1, keepdims=True) + RMS_EPS)
    out_ref[...] = (xo * r) * fgain_ref[...]


def _epilogue(x, z_nat, outs, stats, w_pool, pool_scale, w_pa, w_pb, w_out, b_gate, final_gain):
    b, s, _ = x.shape
    tm = EPI_ROWS
    halo_per_tile = tm // HALO_ROWS
    last_halo = s // HALO_ROWS - 1
    expand = (jnp.arange(LANES)[:, None] == jnp.arange(BRANCH_W)[None, :] // HEAD_DIM).astype(_BF16)

    def full(a):
        return pl.BlockSpec(a.shape, lambda bi, i: (0,) * a.ndim)

    def row(w, col):
        return pl.BlockSpec((None, tm, w), lambda bi, i: (bi, i, col))

    def by_residue(a):
        dil, w = a.shape[1], a.shape[3]
        return pl.BlockSpec((None, dil, tm // dil, w), lambda bi, i: (bi, 0, i, 0))

    in_specs = [
        row(D_MODEL, 0),
        row(2 * BRANCH_W, 0),
        pl.BlockSpec((None, HALO_ROWS, BRANCH_W),
                     lambda bi, i: (bi, jnp.maximum(i * halo_per_tile - 1, 0), 0)),
        pl.BlockSpec((None, HALO_ROWS, BRANCH_W),
                     lambda bi, i: (bi, jnp.minimum((i + 1) * halo_per_tile, last_halo), 0)),
        row(D_MODEL, 1), row(D_MODEL, 2),
        row(BRANCH_W, NAT_ATT_GATE),
        pl.BlockSpec((None, None, tm, BRANCH_W), lambda bi, i: (bi, 0, i, 0)),
        by_residue(outs[1]), by_residue(outs[2]),
        pl.BlockSpec((None, None, tm, LANES), lambda bi, i: (bi, 0, i, 0)),
        by_residue(stats[1]), by_residue(stats[2]),
        full(w_pool), full(pool_scale), full(w_pa), full(w_pb), full(w_out),
        full(b_gate), full(final_gain), full(expand),
    ]
    return pl.pallas_call(
        functools.partial(_epilogue_kernel, seq_len=s),
        grid=(b, s // tm),
        in_specs=in_specs,
        out_specs=pl.BlockSpec((None, tm, D_MODEL), lambda bi, i: (bi, i, 0)),
        out_shape=jax.ShapeDtypeStruct((b, s, D_MODEL), _F32),
        scratch_shapes=[pltpu.VMEM((tm + 2 * HALO_ROWS, BRANCH_W), _F32),
                        pltpu.VMEM((BRANCH_W // LANES, tm, LANES), _F32),
                        pltpu.VMEM((BRANCH_W // LANES, tm, LANES), _F32),
                        pltpu.VMEM((tm, LANES), _F32), pltpu.VMEM((tm, LANES), _F32)],
        compiler_params=pltpu.CompilerParams(
            dimension_semantics=("arbitrary", "arbitrary"), vmem_limit_bytes=VMEM_LIMIT),
        name="epilogue",
    )(x, z_nat, z_nat, z_nat, z_nat, z_nat, z_nat, *outs, *stats,
      w_pool, pool_scale, w_pa, w_pb, w_out, b_gate, final_gain, expand)


def _split_w_in(w):
    def qkv(g):
        q, k, v = (w[:, c + g * BRANCH_W:c + (g + 1) * BRANCH_W] for c in (COL_Q, COL_K, COL_V))
        return [q * SCORE_SCALE, k, v]
    nat = [w[:, :COL_Q], w[:, COL_MERGE:], w[:, COL_ATT_GATE:COL_MERGE]] + qkv(0)
    cat = lambda parts: jnp.concatenate(parts, axis=1).astype(_BF16)
    return cat(nat), cat(qkv(1)), cat(qkv(2))


def kernel(x, norm_gain, w_in, b_gate, rel_bias, w_pool, pool_scale, w_proj_a, w_proj_b,
           w_out, final_gain):
    b, s, d = x.shape
    assert d == D_MODEL and norm_gain.shape[0] == 1, "single-layer block of width D_MODEL"
    assert s % (Q_STEP * DILATED_GROUPS[-1][1]) == 0 and s % EPI_ROWS == 0
    assert [dil for _, dil in DILATED_GROUPS] == [1, 4, 16]
    w_nat, w_d4, w_d16 = _split_w_in(w_in[0])
    z_nat, z_d4, z_d16 = _in_proj(x, norm_gain[0][None, :], w_nat, w_d4, w_d16)
    outs, stats = [], []
    for g, (zg, col0) in enumerate(((z_nat[:, None], NAT_QKV), (z_d4, 0), (z_d16, 0))):
        o, st = _attn_group(zg, _band_bias(rel_bias, g, DILATED_GROUPS[g][1]), col0)
        outs.append(o)
        stats.append(st)
    return _epilogue(x, z_nat, outs, stats, w_pool[0].astype(_BF16), pool_scale[0][None, :],
                     w_proj_a[0].astype(_BF16), w_proj_b[0].astype(_BF16),
                     w_out[0].astype(_BF16), b_gate[0], final_gain[None, :])
```

```python
import functools
import math

import jax
import jax.numpy as jnp
import numpy as np
from jax import lax
from jax.experimental import pallas as pl
from jax.experimental.pallas import tpu as pltpu

D_MODEL = 1024
BRANCH_W = D_MODEL // 2
POOL_WINDOWS = (2, 4, 8, 16)
POOL_GROUP_W = BRANCH_W // len(POOL_WINDOWS)
DILATED_GROUPS = ((128, 1), (512, 4), (2048, 16))
N_ATT_GROUPS = len(DILATED_GROUPS)
HEAD_DIM = 64
HEADS_PER_GROUP = BRANCH_W // HEAD_DIM
QKV_W = N_ATT_GROUPS * BRANCH_W
NUM_BUCKETS = 32
MAX_DISTANCE = 1024
IN_W = 2 * BRANCH_W + 3 * QKV_W + BRANCH_W + 2 * D_MODEL
RMS_EPS = 1e-6
NEG_INF = -1e30
LOG2_E = math.log2(math.e)
SCORE_SCALE = LOG2_E / math.sqrt(HEAD_DIM)

COL_Q = 2 * BRANCH_W
COL_K = COL_Q + QKV_W
COL_V = COL_K + QKV_W
COL_ATT_GATE = COL_V + QKV_W
COL_MERGE = COL_ATT_GATE + BRANCH_W

NAT_ATT_GATE = 6
NAT_QKV = 7
NAT_W = 10 * BRANCH_W
QKV_G_W = 3 * BRANCH_W

N_SIDE = 64
LANES = 128
Q_SUB = 128
K_SUB = Q_SUB + 2 * N_SIDE
Q_STEP = 512
HALO_ROWS = 16
POOL_ROWS = 128
POOL_K = POOL_ROWS + 2 * HALO_ROWS

IN_PROJ_ROWS = 256
IN_PROJ_COLS = 512
EPI_ROWS = 512
EPI_SUB_ROWS = 512
VMEM_LIMIT = 56 * 1024 * 1024

_BF16 = jnp.bfloat16
_F32 = jnp.float32


def _in_proj_kernel(x_ref, g_ref, wn_ref, w4_ref, w16_ref, zn_ref, z4_ref, z16_ref,
                    hf_ref, hn_ref, h4_ref, h16_ref):
    tm = x_ref.shape[0]
    x = x_ref[...]
    r = lax.rsqrt(jnp.mean(x * x, axis=-1, keepdims=True) + RMS_EPS)
    hf = (x * r) * g_ref[...]
    n_slab = D_MODEL // LANES
    for c in range(n_slab):
        hf_ref[c] = hf[:, c * LANES:(c + 1) * LANES]
    hn_ref[...] = hf.astype(_BF16)
    for c in range(NAT_W // IN_PROJ_COLS):
        cs = slice(c * IN_PROJ_COLS, (c + 1) * IN_PROJ_COLS)
        zn_ref[:, cs] = jnp.dot(hn_ref[...], wn_ref[:, cs],
                                preferred_element_type=_F32).astype(_BF16)
    for dil, h_ref, w_ref, z_ref in ((4, h4_ref, w4_ref, z4_ref), (16, h16_ref, w16_ref, z16_ref)):
        rows = tm // dil
        for res in range(dil):
            for c in range(n_slab):
                h_ref[res * rows:(res + 1) * rows, c * LANES:(c + 1) * LANES] = (
                    hf_ref[c, pl.ds(res, rows, stride=dil), :].astype(_BF16))
        for c in range(QKV_G_W // IN_PROJ_COLS):
            cs = slice(c * IN_PROJ_COLS, (c + 1) * IN_PROJ_COLS)
            zc = jnp.dot(h_ref[...], w_ref[:, cs], preferred_element_type=_F32).astype(_BF16)
            for res in range(dil):
                z_ref[res, :, cs] = zc[res * rows:(res + 1) * rows]


def _in_proj(x, gain, w_nat, w_d4, w_d16):
    b, s, _ = x.shape
    tm = IN_PROJ_ROWS
    resident = lambda a: pl.BlockSpec(a.shape, lambda bi, i: (0, 0), pipeline_mode=pl.Buffered(1))
    return pl.pallas_call(
        _in_proj_kernel,
        grid=(b, s // tm),
        in_specs=[
            pl.BlockSpec((None, tm, D_MODEL), lambda bi, i: (bi, i, 0)),
            pl.BlockSpec((1, D_MODEL), lambda bi, i: (0, 0)),
            resident(w_nat), resident(w_d4), resident(w_d16),
        ],
        out_specs=[
            pl.BlockSpec((None, tm, NAT_W), lambda bi, i: (bi, i, 0)),
            pl.BlockSpec((None, 4, tm // 4, QKV_G_W), lambda bi, i: (bi, 0, i, 0)),
            pl.BlockSpec((None, 16, tm // 16, QKV_G_W), lambda bi, i: (bi, 0, i, 0)),
        ],
        out_shape=[
            jax.ShapeDtypeStruct((b, s, NAT_W), _BF16),
            jax.ShapeDtypeStruct((b, 4, s // 4, QKV_G_W), _BF16),
            jax.ShapeDtypeStruct((b, 16, s // 16, QKV_G_W), _BF16),
        ],
        scratch_shapes=[pltpu.VMEM((D_MODEL // LANES, tm, LANES), _F32),
                        pltpu.VMEM((tm, D_MODEL), _BF16),
                        pltpu.VMEM((tm, D_MODEL), _BF16), pltpu.VMEM((tm, D_MODEL), _BF16)],
        compiler_params=pltpu.CompilerParams(
            dimension_semantics=("arbitrary", "arbitrary"), vmem_limit_bytes=VMEM_LIMIT),
        name="in_proj",
    )(x, gain, w_nat, w_d4, w_d16)


def _attn_kernel(q_ref, kp_ref, km_ref, kn_ref, vp_ref, vm_ref, vn_ref, bias_ref,
                 o_ref, st_ref, kwin, vwin, *, n_steps):
    j = pl.program_id(2)
    kwin[0:N_SIDE] = kp_ref[...]
    kwin[N_SIDE:N_SIDE + Q_STEP] = km_ref[...]
    kwin[N_SIDE + Q_STEP:] = kn_ref[...]
    vwin[0:N_SIDE] = vp_ref[...]
    vwin[N_SIDE:N_SIDE + Q_STEP] = vm_ref[...]
    vwin[N_SIDE + Q_STEP:] = vn_ref[...]

    lane = lax.broadcasted_iota(jnp.int32, (Q_SUB, LANES), 1)
    low_head = lane < HEAD_DIM
    n_sub = Q_STEP // Q_SUB

    def sub_block(i, carry):
        r0 = pl.multiple_of(i * Q_SUB, Q_SUB)
        at_seq_start = jnp.logical_and(i == 0, j == 0)
        at_seq_end = jnp.logical_and(i == n_sub - 1, j == n_steps - 1)
        var = jnp.where(at_seq_start, 1, jnp.where(at_seq_end, 2, 0))
        stats = jnp.zeros((Q_SUB, LANES), _F32)
        for hp in range(HEADS_PER_GROUP // 2):
            cs = slice(hp * LANES, (hp + 1) * LANES)
            q2 = q_ref[pl.ds(r0, Q_SUB), cs]
            k2 = kwin[pl.ds(r0, K_SUB), cs]
            v2 = jnp.concatenate([vwin[pl.ds(r0, K_SUB), cs],
                                  jnp.ones((K_SUB, LANES), _BF16)], axis=1)
            zero = jnp.zeros_like(q2)
            qs = jnp.concatenate([jnp.where(low_head, q2, zero),
                                  jnp.where(low_head, zero, q2)], axis=0)
            s = lax.dot_general(qs, k2, (((1,), (1,)), ((), ())),
                                preferred_element_type=_F32)
            s = s + bias_ref[var, hp]
            m = jnp.max(s, axis=-1, keepdims=True)
            p = jnp.exp2(s - m)
            ol = jnp.dot(p.astype(_BF16), v2, preferred_element_type=_F32)
            o, l = ol[:, :LANES], ol[:, LANES:]
            o_ref[pl.ds(r0, Q_SUB), cs] = jnp.where(low_head, o[:Q_SUB], o[Q_SUB:]).astype(_BF16)
            for a in range(2):
                h = 2 * hp + a
                rows = slice(a * Q_SUB, (a + 1) * Q_SUB)
                stats = jnp.where(lane == h, m[rows], stats)
                stats = jnp.where(lane == HEADS_PER_GROUP + h, l[rows], stats)
        st_ref[pl.ds(r0, Q_SUB), :] = stats
        return carry

    lax.fori_loop(0, n_sub, sub_block, 0, unroll=True)


def _attn_group(zg, bias, col0):
    b, dil, sp, _ = zg.shape
    n_steps = sp // Q_STEP
    hb = Q_STEP // N_SIDE
    last_hb = sp // N_SIDE - 1

    def main(col):
        return pl.BlockSpec((None, None, Q_STEP, BRANCH_W), lambda bi, r, j: (bi, r, j, col))

    def prev(col):
        return pl.BlockSpec((None, None, N_SIDE, BRANCH_W),
                            lambda bi, r, j: (bi, r, jnp.maximum(j * hb - 1, 0), col))

    def nxt(col):
        return pl.BlockSpec((None, None, N_SIDE, BRANCH_W),
                            lambda bi, r, j: (bi, r, jnp.minimum((j + 1) * hb, last_hb), col))

    qc, kc, vc = col0, col0 + 1, col0 + 2
    return pl.pallas_call(
        functools.partial(_attn_kernel, n_steps=n_steps),
        grid=(b, dil, n_steps),
        in_specs=[main(qc), prev(kc), main(kc), nxt(kc), prev(vc), main(vc), nxt(vc),
                  pl.BlockSpec(bias.shape, lambda bi, r, j: (0, 0, 0, 0))],
        out_specs=[
            pl.BlockSpec((None, None, Q_STEP, BRANCH_W), lambda bi, r, j: (bi, r, j, 0)),
            pl.BlockSpec((None, None, Q_STEP, LANES), lambda bi, r, j: (bi, r, j, 0)),
        ],
        out_shape=[
            jax.ShapeDtypeStruct((b, dil, sp, BRANCH_W), _BF16),
            jax.ShapeDtypeStruct((b, dil, sp, LANES), _F32),
        ],
        scratch_shapes=[pltpu.VMEM((Q_STEP + 2 * N_SIDE, BRANCH_W), _BF16),
                        pltpu.VMEM((Q_STEP + 2 * N_SIDE, BRANCH_W), _BF16)],
        compiler_params=pltpu.CompilerParams(
            dimension_semantics=("arbitrary", "arbitrary", "arbitrary"),
            vmem_limit_bytes=VMEM_LIMIT),
        name=f"attn_d{dil}",
    )(zg, zg, zg, zg, zg, zg, zg, bias)


def _t5_bucket(rel):
    nb = NUM_BUCKETS // 2
    ret = (rel > 0).astype(jnp.int32) * nb
    n = jnp.abs(rel)
    max_exact = nb // 2
    nf = jnp.maximum(n, 1).astype(_F32)
    large = max_exact + (jnp.log(nf / max_exact) / math.log(MAX_DISTANCE / max_exact)
                         * (nb - max_exact)).astype(jnp.int32)
    large = jnp.minimum(large, nb - 1)
    return ret + jnp.where(n < max_exact, n, large)


def _band_bias(rel_bias, g, dil):
    n_off = 2 * N_SIDE + 1
    offs = jnp.arange(-N_SIDE, N_SIDE + 1, dtype=jnp.int32) * dil
    per_off = rel_bias[_t5_bucket(offs)][:, g * HEADS_PER_GROUP:(g + 1) * HEADS_PER_GROUP]
    per_off = per_off.T.astype(_F32) * LOG2_E
    ext = jnp.concatenate(
        [per_off, jnp.full((HEADS_PER_GROUP, K_SUB + 1 - n_off), NEG_INF, _F32)], axis=1)
    band = jnp.tile(ext, (1, Q_SUB))[:, :Q_SUB * K_SUB].reshape(HEADS_PER_GROUP, Q_SUB, K_SUB)
    kj = jnp.arange(K_SUB, dtype=jnp.int32)[None, None, :]
    first = jnp.where(kj < N_SIDE, NEG_INF, band)
    last = jnp.where(kj >= K_SUB - N_SIDE, NEG_INF, band)
    return jnp.stack([band, first, last], axis=0).reshape(
        3, HEADS_PER_GROUP // 2, 2 * Q_SUB, K_SUB)


def _silu_of_half(h):
    return h + h * jnp.tanh(h)


def _epilogue_kernel(x_ref, za_ref, zprev_ref, znext_ref, zga_ref, zgb_ref, zbg_ref,
                     o1_ref, o4_ref, o16_ref, s1_ref, s4_ref, s16_ref,
                     icount_ref, pscale_ref, bgate_ref, fgain_ref,
                     pmat_in, wpool_in, wpa_in, wpb_in, wout_in, expand_in,
                     out_ref,
                     ext_ref, o4n_ref, o16n_ref, s4n_ref, s16n_ref,
                     pmat_ref, wpool_ref, wpa_ref, wpb_ref, wout_ref, expand_ref, *, seq_len):
    tm = x_ref.shape[0]
    p0 = pl.program_id(1) * tm

    @pl.when(jnp.logical_and(pl.program_id(0) == 0, pl.program_id(1) == 0))
    def _stage_weights():
        for src, dst in ((pmat_in, pmat_ref), (wpool_in, wpool_ref), (wpa_in, wpa_ref),
                         (wpb_in, wpb_ref), (wout_in, wout_ref), (expand_in, expand_ref)):
            dst[...] = src[...]

    has_prev = (p0 > 0).astype(_BF16)
    has_next = (p0 + tm < seq_len).astype(_BF16)
    ext_ref[0:HALO_ROWS] = zprev_ref[...] * has_prev
    ext_ref[HALO_ROWS:HALO_ROWS + tm] = za_ref[:, 0:BRANCH_W]
    ext_ref[HALO_ROWS + tm:] = znext_ref[...] * has_next

    for dil, o_ref, on_ref, s_ref, sn_ref in ((4, o4_ref, o4n_ref, s4_ref, s4n_ref),
                                              (16, o16_ref, o16n_ref, s16_ref, s16n_ref)):
        rows = tm // dil
        for res in range(dil):
            o_res = o_ref[res].astype(_F32)
            for c in range(BRANCH_W // LANES):
                on_ref[c, pl.ds(res, rows, stride=dil), :] = o_res[:, c * LANES:(c + 1) * LANES]
            sn_ref[pl.ds(res, rows, stride=dil), :] = s_ref[res]

    head_lane = lax.broadcasted_iota(jnp.int32, (EPI_SUB_ROWS, LANES), 1) < HEADS_PER_GROUP
    for r0 in range(0, tm, EPI_SUB_ROWS):
        rs = slice(r0, r0 + EPI_SUB_ROWS)

        a_in = za_ref[rs, 0:BRANCH_W].astype(_F32)
        pooled = []
        for gi in range(len(POOL_WINDOWS)):
            cs = slice(gi * POOL_GROUP_W, (gi + 1) * POOL_GROUP_W)
            win = jnp.concatenate(
                [jnp.dot(pmat_ref[gi], ext_ref[q0:q0 + POOL_K, cs], preferred_element_type=_F32)
                 for q0 in range(r0, r0 + EPI_SUB_ROWS, POOL_ROWS)], axis=0)
            pooled.append((win * icount_ref[rs, cs] - a_in[:, cs]).astype(_BF16))
        mixed = [jnp.dot(jnp.concatenate(pooled[2 * pr:2 * pr + 2], axis=-1), wpool_ref[pr],
                         preferred_element_type=_F32) for pr in range(len(POOL_WINDOWS) // 2)]
        mixed = jnp.concatenate(mixed, axis=-1) * pscale_ref[...]
        ya_in = mixed * _silu_of_half(za_ref[rs, BRANCH_W:].astype(_F32))
        y_a = jnp.dot(ya_in.astype(_BF16), wpa_ref[...], preferred_element_type=_F32)

        st = (s1_ref[rs, :], s4n_ref[rs, :], s16n_ref[rs, :])
        mx = jnp.maximum(jnp.maximum(st[0], st[1]), st[2])
        es = [jnp.exp2(t - mx) for t in st]
        ls = [pltpu.roll(t, LANES - HEADS_PER_GROUP, 1) for t in st]
        den = jnp.where(head_lane, es[0] * ls[0] + es[1] * ls[1] + es[2] * ls[2], 1.0)
        inv = 1.0 / den
        packed = jnp.zeros((EPI_SUB_ROWS, LANES), _F32)
        for g, e in enumerate(es):
            wgt = jnp.where(head_lane, e * inv, 0.0)
            hi_part = wgt.astype(_BF16).astype(_F32)
            for part, lane0 in ((hi_part, HEADS_PER_GROUP * g),
                                (wgt - hi_part, HEADS_PER_GROUP * (N_ATT_GROUPS + g))):
                packed = packed + (pltpu.roll(part, lane0, 1) if lane0 else part)
        wide = jnp.dot(packed.astype(_BF16), expand_ref[...], preferred_element_type=_F32)
        ogs = (o1_ref[rs, :].astype(_F32),
               jnp.concatenate([o4n_ref[c, rs, :] for c in range(BRANCH_W // LANES)], axis=-1),
               jnp.concatenate([o16n_ref[c, rs, :] for c in range(BRANCH_W // LANES)], axis=-1))
        att = sum(wide[:, g * BRANCH_W:(g + 1) * BRANCH_W] * og for g, og in enumerate(ogs))
        yb_in = att * _silu_of_half(zbg_ref[rs, :].astype(_F32))
        y_b = jnp.dot(yb_in.astype(_BF16), wpb_ref[...], preferred_element_type=_F32)

        t_a = jnp.tanh(zga_ref[rs, :].astype(_F32) + bgate_ref[0:1, :])
        t_b = jnp.tanh(zgb_ref[rs, :].astype(_F32) + bgate_ref[1:2, :])
        merged2 = (y_a + y_b) + (t_a * y_a + t_b * y_b)
        xo = x_ref[rs, :] + jnp.dot(merged2.astype(_BF16), wout_ref[...],
                                    preferred_element_type=_F32)
        r = lax.rsqrt(jnp.mean(xo * xo, axis=-1, keepdims=True) + RMS_EPS)
        out_ref[rs, :] = (xo * r) * fgain_ref[...]


def _pool_matrices():
    r = np.arange(POOL_ROWS)[:, None]
    c = np.arange(POOL_K)[None, :] - HALO_ROWS
    return np.stack([(c - r >= -(w // 2)) & (c - r <= w // 2 - 1)
                     for w in POOL_WINDOWS]).astype(np.float32)


def _inv_window_counts(seq_len):
    pos = jnp.arange(seq_len, dtype=jnp.int32)[:, None]
    half = jnp.asarray([w // 2 for w in POOL_WINDOWS], jnp.int32)[None, :]
    count = jnp.minimum(pos + half - 1, seq_len - 1) - jnp.maximum(pos - half, 0) + 1
    return jnp.repeat(1.0 / count.astype(_F32), POOL_GROUP_W, axis=1)


def _expand_matrix():
    row = np.arange(LANES)[:, None]
    col = np.arange(N_ATT_GROUPS * BRANCH_W)[None, :]
    n_packed = N_ATT_GROUPS * HEADS_PER_GROUP
    hit = ((row < 2 * n_packed)
           & ((row % n_packed) // HEADS_PER_GROUP == col // BRANCH_W)
           & (row % HEADS_PER_GROUP == (col % BRANCH_W) // HEAD_DIM))
    return hit.astype(np.float32)


def _epilogue(x, z_nat, outs, stats, w_pool, pool_scale, w_pa, w_pb, w_out_half, b_gate_half,
              final_gain):
    b, s, _ = x.shape
    tm = EPI_ROWS
    halo_per_tile = tm // HALO_ROWS
    last_halo = s // HALO_ROWS - 1
    staged = (jnp.asarray(_pool_matrices(), _BF16), w_pool, w_pa, w_pb, w_out_half,
              jnp.asarray(_expand_matrix(), _BF16))

    def full(a):
        return pl.BlockSpec(a.shape, lambda bi, i: (0,) * a.ndim)

    def once(a):
        return pl.BlockSpec(a.shape, lambda bi, i: (0,) * a.ndim, pipeline_mode=pl.Buffered(1))

    def row(w, col):
        return pl.BlockSpec((None, tm, w), lambda bi, i: (bi, i, col))

    def by_residue(a):
        dil, w = a.shape[1], a.shape[3]
        return pl.BlockSpec((None, dil, tm // dil, w), lambda bi, i: (bi, 0, i, 0))

    in_specs = [
        row(D_MODEL, 0),
        row(2 * BRANCH_W, 0),
        pl.BlockSpec((None, HALO_ROWS, BRANCH_W),
                     lambda bi, i: (bi, jnp.maximum(i * halo_per_tile - 1, 0), 0)),
        pl.BlockSpec((None, HALO_ROWS, BRANCH_W),
                     lambda bi, i: (bi, jnp.minimum((i + 1) * halo_per_tile, last_halo), 0)),
        row(D_MODEL, 1), row(D_MODEL, 2),
        row(BRANCH_W, NAT_ATT_GATE),
        pl.BlockSpec((None, None, tm, BRANCH_W), lambda bi, i: (bi, 0, i, 0)),
        by_residue(outs[1]), by_residue(outs[2]),
        pl.BlockSpec((None, None, tm, LANES), lambda bi, i: (bi, 0, i, 0)),
        by_residue(stats[1]), by_residue(stats[2]),
        pl.BlockSpec((tm, BRANCH_W), lambda bi, i: (i, 0)),
        full(pool_scale), full(b_gate_half), full(final_gain),
    ] + [once(a) for a in staged]
    return pl.pallas_call(
        functools.partial(_epilogue_kernel, seq_len=s),
        grid=(b, s // tm),
        in_specs=in_specs,
        out_specs=pl.BlockSpec((None, tm, D_MODEL), lambda bi, i: (bi, i, 0)),
        out_shape=jax.ShapeDtypeStruct((b, s, D_MODEL), _F32),
        scratch_shapes=[pltpu.VMEM((tm + 2 * HALO_ROWS, BRANCH_W), _BF16),
                        pltpu.VMEM((BRANCH_W // LANES, tm, LANES), _F32),
                        pltpu.VMEM((BRANCH_W // LANES, tm, LANES), _F32),
                        pltpu.VMEM((tm, LANES), _F32), pltpu.VMEM((tm, LANES), _F32)]
        + [pltpu.VMEM(a.shape, _BF16) for a in staged],
        compiler_params=pltpu.CompilerParams(
            dimension_semantics=("arbitrary", "arbitrary"), vmem_limit_bytes=VMEM_LIMIT),
        name="epilogue",
    )(x, z_nat, z_nat, z_nat, z_nat, z_nat, z_nat, *outs, *stats,
      _inv_window_counts(s), pool_scale, b_gate_half, final_gain, *staged)


def _split_w_in(w):
    def qkv(g):
        q, k, v = (w[:, c + g * BRANCH_W:c + (g + 1) * BRANCH_W] for c in (COL_Q, COL_K, COL_V))
        return [q * SCORE_SCALE, k, v]
    nat = [w[:, :BRANCH_W], 0.5 * w[:, BRANCH_W:COL_Q], 0.5 * w[:, COL_MERGE:],
           0.5 * w[:, COL_ATT_GATE:COL_MERGE]] + qkv(0)
    cat = lambda parts: jnp.concatenate(parts, axis=1).astype(_BF16)
    return cat(nat), cat(qkv(1)), cat(qkv(2))


def _pair_block_diag(w_pool):
    zero = jnp.zeros_like(w_pool[0])
    pair = lambda a, b: jnp.block([[a, zero], [zero, b]])
    return jnp.stack([pair(w_pool[0], w_pool[1]), pair(w_pool[2], w_pool[3])]).astype(_BF16)


def kernel(x, norm_gain, w_in, b_gate, rel_bias, w_pool, pool_scale, w_proj_a, w_proj_b,
           w_out, final_gain):
    b, s, d = x.shape
    assert d == D_MODEL and norm_gain.shape[0] == 1, "single-layer block of width D_MODEL"
    assert s % (Q_STEP * DILATED_GROUPS[-1][1]) == 0 and s % EPI_ROWS == 0
    assert [dil for _, dil in DILATED_GROUPS] == [1, 4, 16]
    w_nat, w_d4, w_d16 = _split_w_in(w_in[0])
    z_nat, z_d4, z_d16 = _in_proj(x, norm_gain[0][None, :], w_nat, w_d4, w_d16)
    outs, stats = [], []
    for g, (zg, col0) in enumerate(((z_nat[:, None], NAT_QKV), (z_d4, 0), (z_d16, 0))):
        o, st = _attn_group(zg, _band_bias(rel_bias, g, DILATED_GROUPS[g][1]), col0)
        outs.append(o)
        stats.append(st)
    return _epilogue(x, z_nat, outs, stats, _pair_block_diag(w_pool[0]), pool_scale[0][None, :],
                     w_proj_a[0].astype(_BF16), w_proj_b[0].astype(_BF16),
                     (0.5 * w_out[0]).astype(_BF16), 0.5 * b_gate[0], final_gain[None, :])
```

```python
import functools
import math

import jax
import jax.numpy as jnp
import numpy as np
from jax import lax
from jax.experimental import pallas as pl
from jax.experimental.pallas import tpu as pltpu

D_MODEL = 1024
BRANCH_W = D_MODEL // 2
POOL_WINDOWS = (2, 4, 8, 16)
POOL_GROUP_W = BRANCH_W // len(POOL_WINDOWS)
DILATED_GROUPS = ((128, 1), (512, 4), (2048, 16))
N_ATT_GROUPS = len(DILATED_GROUPS)
HEAD_DIM = 64
HEADS_PER_GROUP = BRANCH_W // HEAD_DIM
QKV_W = N_ATT_GROUPS * BRANCH_W
NUM_BUCKETS = 32
MAX_DISTANCE = 1024
IN_W = 2 * BRANCH_W + 3 * QKV_W + BRANCH_W + 2 * D_MODEL
RMS_EPS = 1e-6
NEG_INF = -1e30
LOG2_E = math.log2(math.e)
SCORE_SCALE = LOG2_E / math.sqrt(HEAD_DIM)

COL_Q = 2 * BRANCH_W
COL_K = COL_Q + QKV_W
COL_V = COL_K + QKV_W
COL_ATT_GATE = COL_V + QKV_W
COL_MERGE = COL_ATT_GATE + BRANCH_W

NAT_ATT_GATE = 6
NAT_QKV = 7
NAT_W = 10 * BRANCH_W
QKV_G_W = 3 * BRANCH_W

N_SIDE = 64
LANES = 128
Q_SUB = 128
K_SUB = Q_SUB + 2 * N_SIDE
ATTN_STEP = ((1, 1024), (1, 1024), (2, 512))
HALO_ROWS = 16
POOL_ROWS = 128
POOL_K = POOL_ROWS + 2 * HALO_ROWS

IN_PROJ_ROWS = 512
IN_PROJ_COLS = 512
EPI_ROWS = 512
EPI_SUB_ROWS = 512
VMEM_LIMIT = 56 * 1024 * 1024

_BF16 = jnp.bfloat16
_F32 = jnp.float32


def _in_proj_kernel(x_ref, g_ref, wn_ref, w4_ref, w16_ref, zn_ref, z4_ref, z16_ref,
                    hf_ref, hn_ref, h4_ref, h16_ref):
    tm = x_ref.shape[0]
    x = x_ref[...]
    r = lax.rsqrt(jnp.mean(x * x, axis=-1, keepdims=True) + RMS_EPS)
    hf = (x * r) * g_ref[...]
    n_slab = D_MODEL // LANES
    for c in range(n_slab):
        hf_ref[c] = hf[:, c * LANES:(c + 1) * LANES]
    hn_ref[...] = hf.astype(_BF16)
    for c in range(NAT_W // IN_PROJ_COLS):
        cs = slice(c * IN_PROJ_COLS, (c + 1) * IN_PROJ_COLS)
        zn_ref[:, cs] = jnp.dot(hn_ref[...], wn_ref[:, cs],
                                preferred_element_type=_F32).astype(_BF16)
    for dil, h_ref, w_ref, z_ref in ((4, h4_ref, w4_ref, z4_ref), (16, h16_ref, w16_ref, z16_ref)):
        rows = tm // dil
        for res in range(dil):
            for c in range(n_slab):
                h_ref[res * rows:(res + 1) * rows, c * LANES:(c + 1) * LANES] = (
                    hf_ref[c, pl.ds(res, rows, stride=dil), :].astype(_BF16))
        for c in range(QKV_G_W // IN_PROJ_COLS):
            cs = slice(c * IN_PROJ_COLS, (c + 1) * IN_PROJ_COLS)
            zc = jnp.dot(h_ref[...], w_ref[:, cs], preferred_element_type=_F32).astype(_BF16)
            for res in range(dil):
                z_ref[res, :, cs] = zc[res * rows:(res + 1) * rows]


def _in_proj(x, gain, w_nat, w_d4, w_d16):
    b, s, _ = x.shape
    tm = IN_PROJ_ROWS
    resident = lambda a: pl.BlockSpec(a.shape, lambda bi, i: (0, 0), pipeline_mode=pl.Buffered(1))
    return pl.pallas_call(
        _in_proj_kernel,
        grid=(b, s // tm),
        in_specs=[
            pl.BlockSpec((None, tm, D_MODEL), lambda bi, i: (bi, i, 0)),
            pl.BlockSpec((1, D_MODEL), lambda bi, i: (0, 0)),
            resident(w_nat), resident(w_d4), resident(w_d16),
        ],
        out_specs=[
            pl.BlockSpec((None, tm, NAT_W), lambda bi, i: (bi, i, 0)),
            pl.BlockSpec((None, 4, tm // 4, QKV_G_W), lambda bi, i: (bi, 0, i, 0)),
            pl.BlockSpec((None, 16, tm // 16, QKV_G_W), lambda bi, i: (bi, 0, i, 0)),
        ],
        out_shape=[
            jax.ShapeDtypeStruct((b, s, NAT_W), _BF16),
            jax.ShapeDtypeStruct((b, 4, s // 4, QKV_G_W), _BF16),
            jax.ShapeDtypeStruct((b, 16, s // 16, QKV_G_W), _BF16),
        ],
        scratch_shapes=[pltpu.VMEM((D_MODEL // LANES, tm, LANES), _F32),
                        pltpu.VMEM((tm, D_MODEL), _BF16),
                        pltpu.VMEM((tm, D_MODEL), _BF16), pltpu.VMEM((tm, D_MODEL), _BF16)],
        compiler_params=pltpu.CompilerParams(
            dimension_semantics=("arbitrary", "arbitrary"), vmem_limit_bytes=VMEM_LIMIT),
        name="in_proj",
    )(x, gain, w_nat, w_d4, w_d16)


def _attn_kernel(q_ref, kp_ref, km_ref, kn_ref, vp_ref, vm_ref, vn_ref, bias_ref,
                 o_ref, st_ref, kwin, vwin, *, n_steps):
    n_res, q_step = q_ref.shape[0], q_ref.shape[1]
    j = pl.program_id(2)
    kwin[:, 0:N_SIDE] = kp_ref[...]
    kwin[:, N_SIDE:N_SIDE + q_step] = km_ref[...]
    kwin[:, N_SIDE + q_step:] = kn_ref[...]
    vwin[:, 0:N_SIDE] = vp_ref[...]
    vwin[:, N_SIDE:N_SIDE + q_step] = vm_ref[...]
    vwin[:, N_SIDE + q_step:] = vn_ref[...]

    lane = lax.broadcasted_iota(jnp.int32, (Q_SUB, LANES), 1)
    low_head = lane < HEAD_DIM
    n_sub = q_step // Q_SUB
    ones = jnp.ones((K_SUB, LANES), _BF16)

    for res in range(n_res):
        for i in range(n_sub):
            r0 = i * Q_SUB
            if i == 0:
                var = jnp.where(j == 0, 1, 0)
            elif i == n_sub - 1:
                var = jnp.where(j == n_steps - 1, 2, 0)
            else:
                var = 0
            stats = jnp.zeros((Q_SUB, LANES), _F32)
            for hp in range(HEADS_PER_GROUP // 2):
                cs = slice(hp * LANES, (hp + 1) * LANES)
                q2 = q_ref[res, r0:r0 + Q_SUB, cs]
                k2 = kwin[res, r0:r0 + K_SUB, cs]
                v2 = jnp.concatenate([vwin[res, r0:r0 + K_SUB, cs], ones], axis=1)
                zero = jnp.zeros_like(q2)
                qs = jnp.concatenate([jnp.where(low_head, q2, zero),
                                      jnp.where(low_head, zero, q2)], axis=0)
                s = lax.dot_general(qs, k2, (((1,), (1,)), ((), ())),
                                    preferred_element_type=_F32)
                s = s + bias_ref[var, hp]
                m = jnp.max(s, axis=-1, keepdims=True)
                p = jnp.exp2(s - m)
                ol = jnp.dot(p.astype(_BF16), v2, preferred_element_type=_F32)
                o, l = ol[:, :LANES], ol[:, LANES:]
                o_ref[res, r0:r0 + Q_SUB, cs] = (
                    jnp.where(low_head, o[:Q_SUB], o[Q_SUB:]).astype(_BF16))
                for a in range(2):
                    h = 2 * hp + a
                    rows = slice(a * Q_SUB, (a + 1) * Q_SUB)
                    stats = jnp.where(lane == h, m[rows], stats)
                    stats = jnp.where(lane == HEADS_PER_GROUP + h, l[rows], stats)
            st_ref[res, r0:r0 + Q_SUB, :] = stats


def _attn_group(zg, bias, col0, n_res, q_step):
    b, dil, sp, _ = zg.shape
    assert dil % n_res == 0 and sp % q_step == 0 and q_step >= 2 * Q_SUB
    n_steps = sp // q_step
    hb = q_step // N_SIDE
    last_hb = sp // N_SIDE - 1

    def main(col, width=BRANCH_W):
        return pl.BlockSpec((None, n_res, q_step, width), lambda bi, r, j: (bi, r, j, col))

    def prev(col):
        return pl.BlockSpec((None, n_res, N_SIDE, BRANCH_W),
                            lambda bi, r, j: (bi, r, jnp.maximum(j * hb - 1, 0), col))

    def nxt(col):
        return pl.BlockSpec((None, n_res, N_SIDE, BRANCH_W),
                            lambda bi, r, j: (bi, r, jnp.minimum((j + 1) * hb, last_hb), col))

    qc, kc, vc = col0, col0 + 1, col0 + 2
    window = pltpu.VMEM((n_res, q_step + 2 * N_SIDE, BRANCH_W), _BF16)
    return pl.pallas_call(
        functools.partial(_attn_kernel, n_steps=n_steps),
        grid=(b, dil // n_res, n_steps),
        in_specs=[main(qc), prev(kc), main(kc), nxt(kc), prev(vc), main(vc), nxt(vc),
                  pl.BlockSpec(bias.shape, lambda bi, r, j: (0, 0, 0, 0))],
        out_specs=[main(0), main(0, LANES)],
        out_shape=[
            jax.ShapeDtypeStruct((b, dil, sp, BRANCH_W), _BF16),
            jax.ShapeDtypeStruct((b, dil, sp, LANES), _F32),
        ],
        scratch_shapes=[window, window],
        compiler_params=pltpu.CompilerParams(
            dimension_semantics=("arbitrary", "arbitrary", "arbitrary"),
            vmem_limit_bytes=VMEM_LIMIT),
        name=f"attn_d{dil}",
    )(zg, zg, zg, zg, zg, zg, zg, bias)


def _t5_bucket(rel):
    nb = NUM_BUCKETS // 2
    ret = (rel > 0).astype(jnp.int32) * nb
    n = jnp.abs(rel)
    max_exact = nb // 2
    nf = jnp.maximum(n, 1).astype(_F32)
    large = max_exact + (jnp.log(nf / max_exact) / math.log(MAX_DISTANCE / max_exact)
                         * (nb - max_exact)).astype(jnp.int32)
    large = jnp.minimum(large, nb - 1)
    return ret + jnp.where(n < max_exact, n, large)


def _band_bias(rel_bias, g, dil):
    n_off = 2 * N_SIDE + 1
    offs = jnp.arange(-N_SIDE, N_SIDE + 1, dtype=jnp.int32) * dil
    per_off = rel_bias[_t5_bucket(offs)][:, g * HEADS_PER_GROUP:(g + 1) * HEADS_PER_GROUP]
    per_off = per_off.T.astype(_F32) * LOG2_E
    ext = jnp.concatenate(
        [per_off, jnp.full((HEADS_PER_GROUP, K_SUB + 1 - n_off), NEG_INF, _F32)], axis=1)
    band = jnp.tile(ext, (1, Q_SUB))[:, :Q_SUB * K_SUB].reshape(HEADS_PER_GROUP, Q_SUB, K_SUB)
    kj = jnp.arange(K_SUB, dtype=jnp.int32)[None, None, :]
    first = jnp.where(kj < N_SIDE, NEG_INF, band)
    last = jnp.where(kj >= K_SUB - N_SIDE, NEG_INF, band)
    return jnp.stack([band, first, last], axis=0).reshape(
        3, HEADS_PER_GROUP // 2, 2 * Q_SUB, K_SUB)


def _silu_of_half(h):
    return h + h * jnp.tanh(h)


def _epilogue_kernel(x_ref, za_ref, zprev_ref, znext_ref, zga_ref, zgb_ref, zbg_ref,
                     o1_ref, o4_ref, o16_ref, s1_ref, s4_ref, s16_ref,
                     icount_in, pscale_ref, bgate_ref, fgain_ref,
                     pmat_in, wpool_in, wpa_in, wpb_in, wout_in, expand_in,
                     out_ref,
                     ext_ref, o4n_ref, o16n_ref, s4n_ref, s16n_ref, icount_ref,
                     pmat_ref, wpool_ref, wpa_ref, wpb_ref, wout_ref, expand_ref, *, seq_len):
    tm = x_ref.shape[0]
    p0 = pl.program_id(1) * tm

    @pl.when(jnp.logical_and(pl.program_id(0) == 0, pl.program_id(1) == 0))
    def _stage_weights():
        for src, dst in ((pmat_in, pmat_ref), (wpool_in, wpool_ref), (wpa_in, wpa_ref),
                         (wpb_in, wpb_ref), (wout_in, wout_ref), (expand_in, expand_ref)):
            dst[...] = src[...]
        icount_ref[...] = jnp.broadcast_to(icount_in[0, 0:1, :], icount_ref.shape)

    icount_ref[0:HALO_ROWS] = jnp.where(p0 == 0, icount_in[1], icount_in[0])
    icount_ref[tm - HALO_ROWS:tm] = jnp.where(p0 + tm == seq_len, icount_in[2], icount_in[0])

    has_prev = (p0 > 0).astype(_BF16)
    has_next = (p0 + tm < seq_len).astype(_BF16)
    ext_ref[0:HALO_ROWS] = zprev_ref[...] * has_prev
    ext_ref[HALO_ROWS:HALO_ROWS + tm] = za_ref[:, 0:BRANCH_W]
    ext_ref[HALO_ROWS + tm:] = znext_ref[...] * has_next

    for dil, o_ref, on_ref, s_ref, sn_ref in ((4, o4_ref, o4n_ref, s4_ref, s4n_ref),
                                              (16, o16_ref, o16n_ref, s16_ref, s16n_ref)):
        rows = tm // dil
        for res in range(dil):
            o_res = o_ref[res].astype(_F32)
            for c in range(BRANCH_W // LANES):
                on_ref[c, pl.ds(res, rows, stride=dil), :] = o_res[:, c * LANES:(c + 1) * LANES]
            sn_ref[pl.ds(res, rows, stride=dil), :] = s_ref[res]

    head_lane = lax.broadcasted_iota(jnp.int32, (EPI_SUB_ROWS, LANES), 1) < HEADS_PER_GROUP
    for r0 in range(0, tm, EPI_SUB_ROWS):
        rs = slice(r0, r0 + EPI_SUB_ROWS)

        a_in = za_ref[rs, 0:BRANCH_W].astype(_F32)
        pooled = []
        for gi in range(len(POOL_WINDOWS)):
            cs = slice(gi * POOL_GROUP_W, (gi + 1) * POOL_GROUP_W)
            win = jnp.concatenate(
                [jnp.dot(pmat_ref[gi], ext_ref[q0:q0 + POOL_K, cs], preferred_element_type=_F32)
                 for q0 in range(r0, r0 + EPI_SUB_ROWS, POOL_ROWS)], axis=0)
            pooled.append((win * icount_ref[rs, cs] - a_in[:, cs]).astype(_BF16))
        mixed = [jnp.dot(jnp.concatenate(pooled[2 * pr:2 * pr + 2], axis=-1), wpool_ref[pr],
                         preferred_element_type=_F32) for pr in range(len(POOL_WINDOWS) // 2)]
        mixed = jnp.concatenate(mixed, axis=-1) * pscale_ref[...]
        ya_in = mixed * _silu_of_half(za_ref[rs, BRANCH_W:].astype(_F32))
        y_a = jnp.dot(ya_in.astype(_BF16), wpa_ref[...], preferred_element_type=_F32)

        st = (s1_ref[rs, :], s4n_ref[rs, :], s16n_ref[rs, :])
        mx = jnp.maximum(jnp.maximum(st[0], st[1]), st[2])
        es = [jnp.exp2(t - mx) for t in st]
        ls = [pltpu.roll(t, LANES - HEADS_PER_GROUP, 1) for t in st]
        den = jnp.where(head_lane, es[0] * ls[0] + es[1] * ls[1] + es[2] * ls[2], 1.0)
        inv = 1.0 / den
        packed = jnp.zeros((EPI_SUB_ROWS, LANES), _F32)
        for g, e in enumerate(es):
            wgt = jnp.where(head_lane, e * inv, 0.0)
            hi_part = wgt.astype(_BF16).astype(_F32)
            for part, lane0 in ((hi_part, HEADS_PER_GROUP * g),
                                (wgt - hi_part, HEADS_PER_GROUP * (N_ATT_GROUPS + g))):
                packed = packed + (pltpu.roll(part, lane0, 1) if lane0 else part)
        wide = jnp.dot(packed.astype(_BF16), expand_ref[...], preferred_element_type=_F32)
        ogs = (o1_ref[rs, :].astype(_F32),
               jnp.concatenate([o4n_ref[c, rs, :] for c in range(BRANCH_W // LANES)], axis=-1),
               jnp.concatenate([o16n_ref[c, rs, :] for c in range(BRANCH_W // LANES)], axis=-1))
        att = sum(wide[:, g * BRANCH_W:(g + 1) * BRANCH_W] * og for g, og in enumerate(ogs))
        yb_in = att * _silu_of_half(zbg_ref[rs, :].astype(_F32))
        y_b = jnp.dot(yb_in.astype(_BF16), wpb_ref[...], preferred_element_type=_F32)

        t_a = jnp.tanh(zga_ref[rs, :].astype(_F32) + bgate_ref[0:1, :])
        t_b = jnp.tanh(zgb_ref[rs, :].astype(_F32) + bgate_ref[1:2, :])
        merged2 = (y_a + y_b) + (t_a * y_a + t_b * y_b)
        xo = x_ref[rs, :] + jnp.dot(merged2.astype(_BF16), wout_ref[...],
                                    preferred_element_type=_F32)
        r = lax.rsqrt(jnp.mean(xo * xo, axis=-1, keepdims=True) + RMS_EPS)
        out_ref[rs, :] = (xo * r) * fgain_ref[...]


def _pool_matrices():
    r = np.arange(POOL_ROWS)[:, None]
    c = np.arange(POOL_K)[None, :] - HALO_ROWS
    return np.stack([(c - r >= -(w // 2)) & (c - r <= w // 2 - 1)
                     for w in POOL_WINDOWS]).astype(np.float32)


def _inv_window_counts(seq_len):
    half = np.asarray([w // 2 for w in POOL_WINDOWS])[None, :]
    def rows(pos):
        pos = np.asarray(pos)[:, None]
        count = np.minimum(pos + half - 1, seq_len - 1) - np.maximum(pos - half, 0) + 1
        return np.repeat(1.0 / count, POOL_GROUP_W, axis=1)
    mid = seq_len // 2
    return np.stack([rows(range(mid, mid + HALO_ROWS)), rows(range(HALO_ROWS)),
                     rows(range(seq_len - HALO_ROWS, seq_len))]).astype(np.float32)


def _expand_matrix():
    row = np.arange(LANES)[:, None]
    col = np.arange(N_ATT_GROUPS * BRANCH_W)[None, :]
    n_packed = N_ATT_GROUPS * HEADS_PER_GROUP
    hit = ((row < 2 * n_packed)
           & ((row % n_packed) // HEADS_PER_GROUP == col // BRANCH_W)
           & (row % HEADS_PER_GROUP == (col % BRANCH_W) // HEAD_DIM))
    return hit.astype(np.float32)


def _epilogue(x, z_nat, outs, stats, w_pool, pool_scale, w_pa, w_pb, w_out_half, b_gate_half,
              final_gain):
    b, s, _ = x.shape
    tm = EPI_ROWS
    halo_per_tile = tm // HALO_ROWS
    last_halo = s // HALO_ROWS - 1
    staged = (jnp.asarray(_pool_matrices(), _BF16), w_pool, w_pa, w_pb, w_out_half,
              jnp.asarray(_expand_matrix(), _BF16))

    def full(a):
        return pl.BlockSpec(a.shape, lambda bi, i: (0,) * a.ndim)

    def once(a):
        return pl.BlockSpec(a.shape, lambda bi, i: (0,) * a.ndim, pipeline_mode=pl.Buffered(1))

    def row(w, col):
        return pl.BlockSpec((None, tm, w), lambda bi, i: (bi, i, col))

    def by_residue(a):
        dil, w = a.shape[1], a.shape[3]
        return pl.BlockSpec((None, dil, tm // dil, w), lambda bi, i: (bi, 0, i, 0))

    in_specs = [
        row(D_MODEL, 0),
        row(2 * BRANCH_W, 0),
        pl.BlockSpec((None, HALO_ROWS, BRANCH_W),
                     lambda bi, i: (bi, jnp.maximum(i * halo_per_tile - 1, 0), 0)),
        pl.BlockSpec((None, HALO_ROWS, BRANCH_W),
                     lambda bi, i: (bi, jnp.minimum((i + 1) * halo_per_tile, last_halo), 0)),
        row(D_MODEL, 1), row(D_MODEL, 2),
        row(BRANCH_W, NAT_ATT_GATE),
        pl.BlockSpec((None, None, tm, BRANCH_W), lambda bi, i: (bi, 0, i, 0)),
        by_residue(outs[1]), by_residue(outs[2]),
        pl.BlockSpec((None, None, tm, LANES), lambda bi, i: (bi, 0, i, 0)),
        by_residue(stats[1]), by_residue(stats[2]),
        pl.BlockSpec((3, HALO_ROWS, BRANCH_W), lambda bi, i: (0, 0, 0)),
        full(pool_scale), full(b_gate_half), full(final_gain),
    ] + [once(a) for a in staged]
    return pl.pallas_call(
        functools.partial(_epilogue_kernel, seq_len=s),
        grid=(b, s // tm),
        in_specs=in_specs,
        out_specs=pl.BlockSpec((None, tm, D_MODEL), lambda bi, i: (bi, i, 0)),
        out_shape=jax.ShapeDtypeStruct((b, s, D_MODEL), _F32),
        scratch_shapes=[pltpu.VMEM((tm + 2 * HALO_ROWS, BRANCH_W), _BF16),
                        pltpu.VMEM((BRANCH_W // LANES, tm, LANES), _F32),
                        pltpu.VMEM((BRANCH_W // LANES, tm, LANES), _F32),
                        pltpu.VMEM((tm, LANES), _F32), pltpu.VMEM((tm, LANES), _F32),
                        pltpu.VMEM((tm, BRANCH_W), _F32)]
        + [pltpu.VMEM(a.shape, _BF16) for a in staged],
        compiler_params=pltpu.CompilerParams(
            dimension_semantics=("arbitrary", "arbitrary"), vmem_limit_bytes=VMEM_LIMIT),
        name="epilogue",
    )(x, z_nat, z_nat, z_nat, z_nat, z_nat, z_nat, *outs, *stats,
      jnp.asarray(_inv_window_counts(s)), pool_scale, b_gate_half, final_gain, *staged)


def _split_w_in(w):
    def qkv(g):
        q, k, v = (w[:, c + g * BRANCH_W:c + (g + 1) * BRANCH_W] for c in (COL_Q, COL_K, COL_V))
        return [q * SCORE_SCALE, k, v]
    nat = [w[:, :BRANCH_W], 0.5 * w[:, BRANCH_W:COL_Q], 0.5 * w[:, COL_MERGE:],
           0.5 * w[:, COL_ATT_GATE:COL_MERGE]] + qkv(0)
    cat = lambda parts: jnp.concatenate(parts, axis=1).astype(_BF16)
    return cat(nat), cat(qkv(1)), cat(qkv(2))


def _pair_block_diag(w_pool):
    zero = jnp.zeros_like(w_pool[0])
    pair = lambda a, b: jnp.block([[a, zero], [zero, b]])
    return jnp.stack([pair(w_pool[0], w_pool[1]), pair(w_pool[2], w_pool[3])]).astype(_BF16)


def kernel(x, norm_gain, w_in, b_gate, rel_bias, w_pool, pool_scale, w_proj_a, w_proj_b,
           w_out, final_gain):
    b, s, d = x.shape
    assert d == D_MODEL and norm_gain.shape[0] == 1, "single-layer block of width D_MODEL"
    assert s % EPI_ROWS == 0 and s % IN_PROJ_ROWS == 0
    assert [dil for _, dil in DILATED_GROUPS] == [1, 4, 16]
    w_nat, w_d4, w_d16 = _split_w_in(w_in[0])
    z_nat, z_d4, z_d16 = _in_proj(x, norm_gain[0][None, :], w_nat, w_d4, w_d16)
    outs, stats = [], []
    for g, (zg, col0) in enumerate(((z_nat[:, None], NAT_QKV), (z_d4, 0), (z_d16, 0))):
        o, st = _attn_group(zg, _band_bias(rel_bias, g, DILATED_GROUPS[g][1]), col0,
                            *ATTN_STEP[g])
        outs.append(o)
        stats.append(st)
    return _epilogue(x, z_nat, outs, stats, _pair_block_diag(w_pool[0]), pool_scale[0][None, :],
                     w_proj_a[0].astype(_BF16), w_proj_b[0].astype(_BF16),
                     (0.5 * w_out[0]).astype(_BF16), 0.5 * b_gate[0], final_gain[None, :])
```

```python
import functools
import math

import jax
import jax.numpy as jnp
import numpy as np
from jax import lax
from jax.experimental import pallas as pl
from jax.experimental.pallas import tpu as pltpu

D_MODEL = 1024
BRANCH_W = D_MODEL // 2
POOL_WINDOWS = (2, 4, 8, 16)
POOL_GROUP_W = BRANCH_W // len(POOL_WINDOWS)
DILATED_GROUPS = ((128, 1), (512, 4), (2048, 16))
N_ATT_GROUPS = len(DILATED_GROUPS)
HEAD_DIM = 64
HEADS_PER_GROUP = BRANCH_W // HEAD_DIM
QKV_W = N_ATT_GROUPS * BRANCH_W
NUM_BUCKETS = 32
MAX_DISTANCE = 1024
IN_W = 2 * BRANCH_W + 3 * QKV_W + BRANCH_W + 2 * D_MODEL
RMS_EPS = 1e-6
NEG_INF = -1e30
LOG2_E = math.log2(math.e)
SCORE_SCALE = LOG2_E / math.sqrt(HEAD_DIM)

COL_Q = 2 * BRANCH_W
COL_K = COL_Q + QKV_W
COL_V = COL_K + QKV_W
COL_ATT_GATE = COL_V + QKV_W
COL_MERGE = COL_ATT_GATE + BRANCH_W

NAT_ATT_GATE = 6
NAT_QKV = 7
NAT_W = 10 * BRANCH_W
QKV_G_W = 3 * BRANCH_W

N_SIDE = 64
LANES = 128
Q_SUB = 128
K_SUB = Q_SUB + 2 * N_SIDE
ATTN_STEP = ((1, 2048), (1, 2048), (4, 512))
HALO_ROWS = 16
POOL_ROWS = 128
POOL_K = POOL_ROWS + 2 * HALO_ROWS

IN_PROJ_ROWS = 512
IN_PROJ_COLS = 512
EPI_ROWS = 512
VMEM_LIMIT = 56 * 1024 * 1024

_BF16 = jnp.bfloat16
_F32 = jnp.float32


def _pack_rows(w):
    k, n = w.shape
    bits = lax.bitcast_convert_type(w.reshape(k // 2, 2, n), jnp.uint16).astype(jnp.uint32)
    return bits[:, 0] | (bits[:, 1] << 16)


def _unpack_rows(w_packed):
    return pltpu.bitcast(w_packed, _BF16)


def _in_proj_kernel(x_ref, g_ref, wn_ref, w4_ref, w16_ref, zn_ref, z4_ref, z16_ref,
                    hf_ref, hn_ref, h4_ref, h16_ref):
    tm = x_ref.shape[0]
    x = x_ref[...]
    r = lax.rsqrt(jnp.mean(x * x, axis=-1, keepdims=True) + RMS_EPS)
    hf = (x * r) * g_ref[...]
    n_slab = D_MODEL // LANES
    for c in range(n_slab):
        hf_ref[c] = hf[:, c * LANES:(c + 1) * LANES]
    hn_ref[...] = hf.astype(_BF16)
    for c in range(NAT_W // IN_PROJ_COLS):
        cs = slice(c * IN_PROJ_COLS, (c + 1) * IN_PROJ_COLS)
        zn_ref[:, cs] = jnp.dot(hn_ref[...], _unpack_rows(wn_ref[:, cs]),
                                preferred_element_type=_F32).astype(_BF16)
    for dil, h_ref, w_ref, z_ref in ((4, h4_ref, w4_ref, z4_ref), (16, h16_ref, w16_ref, z16_ref)):
        rows = tm // dil
        for res in range(dil):
            for c in range(n_slab):
                h_ref[res * rows:(res + 1) * rows, c * LANES:(c + 1) * LANES] = (
                    hf_ref[c, pl.ds(res, rows, stride=dil), :].astype(_BF16))
        for c in range(QKV_G_W // IN_PROJ_COLS):
            cs = slice(c * IN_PROJ_COLS, (c + 1) * IN_PROJ_COLS)
            zc = jnp.dot(h_ref[...], _unpack_rows(w_ref[:, cs]),
                         preferred_element_type=_F32).astype(_BF16)
            for res in range(dil):
                z_ref[res, :, cs] = zc[res * rows:(res + 1) * rows]


def _in_proj(x, gain, w_nat, w_d4, w_d16):
    b, s, _ = x.shape
    tm = IN_PROJ_ROWS
    resident = lambda a: pl.BlockSpec(a.shape, lambda bi, i: (0, 0), pipeline_mode=pl.Buffered(1))
    return pl.pallas_call(
        _in_proj_kernel,
        grid=(b, s // tm),
        in_specs=[
            pl.BlockSpec((None, tm, D_MODEL), lambda bi, i: (bi, i, 0)),
            pl.BlockSpec((1, D_MODEL), lambda bi, i: (0, 0)),
            resident(w_nat), resident(w_d4), resident(w_d16),
        ],
        out_specs=[
            pl.BlockSpec((None, tm, NAT_W), lambda bi, i: (bi, i, 0)),
            pl.BlockSpec((None, 4, tm // 4, QKV_G_W), lambda bi, i: (bi, 0, i, 0)),
            pl.BlockSpec((None, 16, tm // 16, QKV_G_W), lambda bi, i: (bi, 0, i, 0)),
        ],
        out_shape=[
            jax.ShapeDtypeStruct((b, s, NAT_W), _BF16),
            jax.ShapeDtypeStruct((b, 4, s // 4, QKV_G_W), _BF16),
            jax.ShapeDtypeStruct((b, 16, s // 16, QKV_G_W), _BF16),
        ],
        scratch_shapes=[pltpu.VMEM((D_MODEL // LANES, tm, LANES), _F32),
                        pltpu.VMEM((tm, D_MODEL), _BF16),
                        pltpu.VMEM((tm, D_MODEL), _BF16), pltpu.VMEM((tm, D_MODEL), _BF16)],
        compiler_params=pltpu.CompilerParams(
            dimension_semantics=("arbitrary", "arbitrary"), vmem_limit_bytes=VMEM_LIMIT),
        name="in_proj",
    )(x, gain, w_nat, w_d4, w_d16)


def _attn_kernel(q_ref, kp_ref, km_ref, kn_ref, vp_ref, vm_ref, vn_ref, bias_ref,
                 o_ref, st_ref, kwin, vwin, *, n_steps):
    n_res, q_step = q_ref.shape[0], q_ref.shape[1]
    j = pl.program_id(2)
    kwin[:, 0:N_SIDE] = kp_ref[...]
    kwin[:, N_SIDE:N_SIDE + q_step] = km_ref[...]
    kwin[:, N_SIDE + q_step:] = kn_ref[...]
    vwin[:, 0:N_SIDE] = vp_ref[...]
    vwin[:, N_SIDE:N_SIDE + q_step] = vm_ref[...]
    vwin[:, N_SIDE + q_step:] = vn_ref[...]

    lane = lax.broadcasted_iota(jnp.int32, (Q_SUB, LANES), 1)
    low_head = lane < HEAD_DIM
    n_sub = q_step // Q_SUB
    ones = jnp.ones((K_SUB, LANES), _BF16)

    for res in range(n_res):
        for i in range(n_sub):
            r0 = i * Q_SUB
            if i == 0:
                var = jnp.where(j == 0, 1, 0)
            elif i == n_sub - 1:
                var = jnp.where(j == n_steps - 1, 2, 0)
            else:
                var = 0
            stats = jnp.zeros((Q_SUB, LANES), _F32)
            for hp in range(HEADS_PER_GROUP // 2):
                cs = slice(hp * LANES, (hp + 1) * LANES)
                q2 = q_ref[res, r0:r0 + Q_SUB, cs]
                k2 = kwin[res, r0:r0 + K_SUB, cs]
                v2 = jnp.concatenate([vwin[res, r0:r0 + K_SUB, cs], ones], axis=1)
                zero = jnp.zeros_like(q2)
                qs = jnp.concatenate([jnp.where(low_head, q2, zero),
                                      jnp.where(low_head, zero, q2)], axis=0)
                s = lax.dot_general(qs, k2, (((1,), (1,)), ((), ())),
                                    preferred_element_type=_F32)
                s = s + bias_ref[var, hp]
                m = jnp.max(s, axis=-1, keepdims=True)
                p = jnp.exp2(s - m)
                ol = jnp.dot(p.astype(_BF16), v2, preferred_element_type=_F32)
                o, l = ol[:, :LANES], ol[:, LANES:]
                o_ref[res, r0:r0 + Q_SUB, cs] = (
                    jnp.where(low_head, o[:Q_SUB], o[Q_SUB:]).astype(_BF16))
                for a in range(2):
                    h = 2 * hp + a
                    rows = slice(a * Q_SUB, (a + 1) * Q_SUB)
                    stats = jnp.where(lane == h, m[rows], stats)
                    stats = jnp.where(lane == HEADS_PER_GROUP + h, l[rows], stats)
            st_ref[res, r0:r0 + Q_SUB, :] = stats


def _attn_group(zg, bias, col0, n_res, q_step):
    b, dil, sp, _ = zg.shape
    assert dil % n_res == 0 and sp % q_step == 0 and q_step >= 2 * Q_SUB
    n_steps = sp // q_step
    hb = q_step // N_SIDE
    last_hb = sp // N_SIDE - 1

    def main(col, width=BRANCH_W):
        return pl.BlockSpec((None, n_res, q_step, width), lambda bi, r, j: (bi, r, j, col))

    def prev(col):
        return pl.BlockSpec((None, n_res, N_SIDE, BRANCH_W),
                            lambda bi, r, j: (bi, r, jnp.maximum(j * hb - 1, 0), col))

    def nxt(col):
        return pl.BlockSpec((None, n_res, N_SIDE, BRANCH_W),
                            lambda bi, r, j: (bi, r, jnp.minimum((j + 1) * hb, last_hb), col))

    qc, kc, vc = col0, col0 + 1, col0 + 2
    window = pltpu.VMEM((n_res, q_step + 2 * N_SIDE, BRANCH_W), _BF16)
    return pl.pallas_call(
        functools.partial(_attn_kernel, n_steps=n_steps),
        grid=(b, dil // n_res, n_steps),
        in_specs=[main(qc), prev(kc), main(kc), nxt(kc), prev(vc), main(vc), nxt(vc),
                  pl.BlockSpec(bias.shape, lambda bi, r, j: (0, 0, 0, 0))],
        out_specs=[main(0), main(0, LANES)],
        out_shape=[
            jax.ShapeDtypeStruct((b, dil, sp, BRANCH_W), _BF16),
            jax.ShapeDtypeStruct((b, dil, sp, LANES), _F32),
        ],
        scratch_shapes=[window, window],
        compiler_params=pltpu.CompilerParams(
            dimension_semantics=("arbitrary", "arbitrary", "arbitrary"),
            vmem_limit_bytes=VMEM_LIMIT),
        name=f"attn_d{dil}",
    )(zg, zg, zg, zg, zg, zg, zg, bias)


def _t5_bucket(rel):
    nb = NUM_BUCKETS // 2
    ret = (rel > 0).astype(jnp.int32) * nb
    n = jnp.abs(rel)
    max_exact = nb // 2
    nf = jnp.maximum(n, 1).astype(_F32)
    large = max_exact + (jnp.log(nf / max_exact) / math.log(MAX_DISTANCE / max_exact)
                         * (nb - max_exact)).astype(jnp.int32)
    large = jnp.minimum(large, nb - 1)
    return ret + jnp.where(n < max_exact, n, large)


def _band_bias(rel_bias, g, dil):
    n_off = 2 * N_SIDE + 1
    offs = jnp.arange(-N_SIDE, N_SIDE + 1, dtype=jnp.int32) * dil
    per_off = rel_bias[_t5_bucket(offs)][:, g * HEADS_PER_GROUP:(g + 1) * HEADS_PER_GROUP]
    per_off = per_off.T.astype(_F32) * LOG2_E
    ext = jnp.concatenate(
        [per_off, jnp.full((HEADS_PER_GROUP, K_SUB + 1 - n_off), NEG_INF, _F32)], axis=1)
    band = jnp.tile(ext, (1, Q_SUB))[:, :Q_SUB * K_SUB].reshape(HEADS_PER_GROUP, Q_SUB, K_SUB)
    kj = jnp.arange(K_SUB, dtype=jnp.int32)[None, None, :]
    first = jnp.where(kj < N_SIDE, NEG_INF, band)
    last = jnp.where(kj >= K_SUB - N_SIDE, NEG_INF, band)
    return jnp.stack([band, first, last], axis=0).reshape(
        3, HEADS_PER_GROUP // 2, 2 * Q_SUB, K_SUB)


def _silu_of_half(h):
    return h + h * jnp.tanh(h)


def _epilogue_kernel(x_ref, za_ref, zprev_ref, znext_ref, zga_ref, zgb_ref, zbg_ref,
                     o1_ref, o4_ref, o16_ref, s1_ref, s4_ref, s16_ref,
                     icount_in, pscale_ref, bgate_ref, fgain_ref,
                     pmat_in, wpool_in, wpa_in, wpb_in, wout_in, expand_in,
                     out_ref,
                     ext_ref, o4n_ref, o16n_ref, s4n_ref, s16n_ref, icount_ref,
                     pmat_ref, wpool_ref, wpa_ref, wpb_ref, wout_ref, expand_ref, *, seq_len):
    tm = x_ref.shape[0]
    p0 = pl.program_id(1) * tm

    @pl.when(jnp.logical_and(pl.program_id(0) == 0, pl.program_id(1) == 0))
    def _stage_weights():
        for src, dst in ((pmat_in, pmat_ref), (wpool_in, wpool_ref), (wpa_in, wpa_ref),
                         (wpb_in, wpb_ref), (wout_in, wout_ref), (expand_in, expand_ref)):
            dst[...] = src[...]
        icount_ref[...] = jnp.broadcast_to(icount_in[0, 0:1, :], icount_ref.shape)

    icount_ref[0:HALO_ROWS] = jnp.where(p0 == 0, icount_in[1], icount_in[0])
    icount_ref[tm - HALO_ROWS:tm] = jnp.where(p0 + tm == seq_len, icount_in[2], icount_in[0])

    has_prev = (p0 > 0).astype(_BF16)
    has_next = (p0 + tm < seq_len).astype(_BF16)
    ext_ref[0:HALO_ROWS] = zprev_ref[...] * has_prev
    ext_ref[HALO_ROWS:HALO_ROWS + tm] = za_ref[:, 0:BRANCH_W]
    ext_ref[HALO_ROWS + tm:] = znext_ref[...] * has_next

    for dil, o_ref, on_ref, s_ref, sn_ref in ((4, o4_ref, o4n_ref, s4_ref, s4n_ref),
                                              (16, o16_ref, o16n_ref, s16_ref, s16n_ref)):
        rows = tm // dil
        for res in range(dil):
            o_res = o_ref[res].astype(_F32)
            for c in range(BRANCH_W // LANES):
                on_ref[c, pl.ds(res, rows, stride=dil), :] = o_res[:, c * LANES:(c + 1) * LANES]
            sn_ref[pl.ds(res, rows, stride=dil), :] = s_ref[res]

    a_in = za_ref[:, 0:BRANCH_W].astype(_F32)
    pooled = []
    for gi in range(len(POOL_WINDOWS)):
        cs = slice(gi * POOL_GROUP_W, (gi + 1) * POOL_GROUP_W)
        win = jnp.concatenate(
            [jnp.dot(pmat_ref[gi], ext_ref[q0:q0 + POOL_K, cs], preferred_element_type=_F32)
             for q0 in range(0, tm, POOL_ROWS)], axis=0)
        pooled.append((win * icount_ref[:, cs] - a_in[:, cs]).astype(_BF16))
    mixed = [jnp.dot(jnp.concatenate(pooled[2 * pr:2 * pr + 2], axis=-1), wpool_ref[pr],
                     preferred_element_type=_F32) for pr in range(len(POOL_WINDOWS) // 2)]
    mixed = jnp.concatenate(mixed, axis=-1) * pscale_ref[...]
    ya_in = mixed * _silu_of_half(za_ref[:, BRANCH_W:].astype(_F32))
    y_a = jnp.dot(ya_in.astype(_BF16), wpa_ref[...], preferred_element_type=_F32)

    st = (s1_ref[...], s4n_ref[...], s16n_ref[...])
    mx = jnp.maximum(jnp.maximum(st[0], st[1]), st[2])
    es = [jnp.exp2(t - mx) for t in st]
    ls = [pltpu.roll(t, LANES - HEADS_PER_GROUP, 1) for t in st]
    head_lane = lax.broadcasted_iota(jnp.int32, (tm, LANES), 1) < HEADS_PER_GROUP
    den = jnp.where(head_lane, es[0] * ls[0] + es[1] * ls[1] + es[2] * ls[2], 1.0)
    inv = 1.0 / den
    packed = jnp.zeros((tm, LANES), _F32)
    for g, e in enumerate(es):
        wgt = jnp.where(head_lane, e * inv, 0.0)
        hi_part = wgt.astype(_BF16).astype(_F32)
        for part, lane0 in ((hi_part, HEADS_PER_GROUP * g),
                            (wgt - hi_part, HEADS_PER_GROUP * (N_ATT_GROUPS + g))):
            packed = packed + (pltpu.roll(part, lane0, 1) if lane0 else part)
    wide = jnp.dot(packed.astype(_BF16), expand_ref[...], preferred_element_type=_F32)
    ogs = (o1_ref[...].astype(_F32),
           jnp.concatenate([o4n_ref[c] for c in range(BRANCH_W // LANES)], axis=-1),
           jnp.concatenate([o16n_ref[c] for c in range(BRANCH_W // LANES)], axis=-1))
    att = sum(wide[:, g * BRANCH_W:(g + 1) * BRANCH_W] * og for g, og in enumerate(ogs))
    yb_in = att * _silu_of_half(zbg_ref[...].astype(_F32))
    y_b = jnp.dot(yb_in.astype(_BF16), wpb_ref[...], preferred_element_type=_F32)

    t_a = jnp.tanh(zga_ref[...].astype(_F32) + bgate_ref[0:1, :])
    t_b = jnp.tanh(zgb_ref[...].astype(_F32) + bgate_ref[1:2, :])
    merged2 = (y_a + y_b) + (t_a * y_a + t_b * y_b)
    xo = x_ref[...] + jnp.dot(merged2.astype(_BF16), wout_ref[...], preferred_element_type=_F32)
    r = lax.rsqrt(jnp.mean(xo * xo, axis=-1, keepdims=True) + RMS_EPS)
    out_ref[...] = (xo * r) * fgain_ref[...]


def _pool_matrices():
    r = np.arange(POOL_ROWS)[:, None]
    c = np.arange(POOL_K)[None, :] - HALO_ROWS
    return np.stack([(c - r >= -(w // 2)) & (c - r <= w // 2 - 1)
                     for w in POOL_WINDOWS]).astype(np.float32)


def _inv_window_counts(seq_len):
    half = np.asarray([w // 2 for w in POOL_WINDOWS])[None, :]
    def rows(pos):
        pos = np.asarray(pos)[:, None]
        count = np.minimum(pos + half - 1, seq_len - 1) - np.maximum(pos - half, 0) + 1
        return np.repeat(1.0 / count, POOL_GROUP_W, axis=1)
    mid = seq_len // 2
    return np.stack([rows(range(mid, mid + HALO_ROWS)), rows(range(HALO_ROWS)),
                     rows(range(seq_len - HALO_ROWS, seq_len))]).astype(np.float32)


def _expand_matrix():
    row = np.arange(LANES)[:, None]
    col = np.arange(N_ATT_GROUPS * BRANCH_W)[None, :]
    n_packed = N_ATT_GROUPS * HEADS_PER_GROUP
    hit = ((row < 2 * n_packed)
           & ((row % n_packed) // HEADS_PER_GROUP == col // BRANCH_W)
           & (row % HEADS_PER_GROUP == (col % BRANCH_W) // HEAD_DIM))
    return hit.astype(np.float32)


def _epilogue(x, z_nat, outs, stats, w_pool, pool_scale, w_pa, w_pb, w_out_half, b_gate_half,
              final_gain):
    b, s, _ = x.shape
    tm = EPI_ROWS
    halo_per_tile = tm // HALO_ROWS
    last_halo = s // HALO_ROWS - 1
    staged = (jnp.asarray(_pool_matrices(), _BF16), w_pool, w_pa, w_pb, w_out_half,
              jnp.asarray(_expand_matrix(), _BF16))

    def full(a):
        return pl.BlockSpec(a.shape, lambda bi, i: (0,) * a.ndim)

    def once(a):
        return pl.BlockSpec(a.shape, lambda bi, i: (0,) * a.ndim, pipeline_mode=pl.Buffered(1))

    def row(w, col):
        return pl.BlockSpec((None, tm, w), lambda bi, i: (bi, i, col))

    def by_residue(a):
        dil, w = a.shape[1], a.shape[3]
        return pl.BlockSpec((None, dil, tm // dil, w), lambda bi, i: (bi, 0, i, 0))

    in_specs = [
        row(D_MODEL, 0),
        row(2 * BRANCH_W, 0),
        pl.BlockSpec((None, HALO_ROWS, BRANCH_W),
                     lambda bi, i: (bi, jnp.maximum(i * halo_per_tile - 1, 0), 0)),
        pl.BlockSpec((None, HALO_ROWS, BRANCH_W),
                     lambda bi, i: (bi, jnp.minimum((i + 1) * halo_per_tile, last_halo), 0)),
        row(D_MODEL, 1), row(D_MODEL, 2),
        row(BRANCH_W, NAT_ATT_GATE),
        pl.BlockSpec((None, None, tm, BRANCH_W), lambda bi, i: (bi, 0, i, 0)),
        by_residue(outs[1]), by_residue(outs[2]),
        pl.BlockSpec((None, None, tm, LANES), lambda bi, i: (bi, 0, i, 0)),
        by_residue(stats[1]), by_residue(stats[2]),
        pl.BlockSpec((3, HALO_ROWS, BRANCH_W), lambda bi, i: (0, 0, 0)),
        full(pool_scale), full(b_gate_half), full(final_gain),
    ] + [once(a) for a in staged]
    return pl.pallas_call(
        functools.partial(_epilogue_kernel, seq_len=s),
        grid=(b, s // tm),
        in_specs=in_specs,
        out_specs=pl.BlockSpec((None, tm, D_MODEL), lambda bi, i: (bi, i, 0)),
        out_shape=jax.ShapeDtypeStruct((b, s, D_MODEL), _F32),
        scratch_shapes=[pltpu.VMEM((tm + 2 * HALO_ROWS, BRANCH_W), _BF16),
                        pltpu.VMEM((BRANCH_W // LANES, tm, LANES), _F32),
                        pltpu.VMEM((BRANCH_W // LANES, tm, LANES), _F32),
                        pltpu.VMEM((tm, LANES), _F32), pltpu.VMEM((tm, LANES), _F32),
                        pltpu.VMEM((tm, BRANCH_W), _F32)]
        + [pltpu.VMEM(a.shape, _BF16) for a in staged],
        compiler_params=pltpu.CompilerParams(
            dimension_semantics=("arbitrary", "arbitrary"), vmem_limit_bytes=VMEM_LIMIT),
        name="epilogue",
    )(x, z_nat, z_nat, z_nat, z_nat, z_nat, z_nat, *outs, *stats,
      jnp.asarray(_inv_window_counts(s)), pool_scale, b_gate_half, final_gain, *staged)


def _split_w_in(w):
    def qkv(g):
        q, k, v = (w[:, c + g * BRANCH_W:c + (g + 1) * BRANCH_W] for c in (COL_Q, COL_K, COL_V))
        return [q * SCORE_SCALE, k, v]
    nat = [w[:, :BRANCH_W], 0.5 * w[:, BRANCH_W:COL_Q], 0.5 * w[:, COL_MERGE:],
           0.5 * w[:, COL_ATT_GATE:COL_MERGE]] + qkv(0)
    cat = lambda parts: _pack_rows(jnp.concatenate(parts, axis=1).astype(_BF16))
    return cat(nat), cat(qkv(1)), cat(qkv(2))


def _pair_block_diag(w_pool):
    zero = jnp.zeros_like(w_pool[0])
    pair = lambda a, b: jnp.block([[a, zero], [zero, b]])
    return jnp.stack([pair(w_pool[0], w_pool[1]), pair(w_pool[2], w_pool[3])]).astype(_BF16)


def kernel(x, norm_gain, w_in, b_gate, rel_bias, w_pool, pool_scale, w_proj_a, w_proj_b,
           w_out, final_gain):
    b, s, d = x.shape
    assert d == D_MODEL and norm_gain.shape[0] == 1, "single-layer block of width D_MODEL"
    assert s % EPI_ROWS == 0 and s % IN_PROJ_ROWS == 0
    assert [dil for _, dil in DILATED_GROUPS] == [1, 4, 16]
    w_nat, w_d4, w_d16 = _split_w_in(w_in[0])
    z_nat, z_d4, z_d16 = _in_proj(x, norm_gain[0][None, :], w_nat, w_d4, w_d16)
    outs, stats = [], []
    for g, (zg, col0) in enumerate(((z_nat[:, None], NAT_QKV), (z_d4, 0), (z_d16, 0))):
        o, st = _attn_group(zg, _band_bias(rel_bias, g, DILATED_GROUPS[g][1]), col0,
                            *ATTN_STEP[g])
        outs.append(o)
        stats.append(st)
    return _epilogue(x, z_nat, outs, stats, _pair_block_diag(w_pool[0]), pool_scale[0][None, :],
                     w_proj_a[0].astype(_BF16), w_proj_b[0].astype(_BF16),
                     (0.5 * w_out[0]).astype(_BF16), 0.5 * b_gate[0], final_gain[None, :])
```

```python
import functools
import math

import jax
import jax.numpy as jnp
import numpy as np
from jax import lax
from jax.experimental import pallas as pl
from jax.experimental.pallas import tpu as pltpu

D_MODEL = 1024
BRANCH_W = D_MODEL // 2
POOL_WINDOWS = (2, 4, 8, 16)
POOL_GROUP_W = BRANCH_W // len(POOL_WINDOWS)
DILATED_GROUPS = ((128, 1), (512, 4), (2048, 16))
N_ATT_GROUPS = len(DILATED_GROUPS)
HEAD_DIM = 64
HEADS_PER_GROUP = BRANCH_W // HEAD_DIM
QKV_W = N_ATT_GROUPS * BRANCH_W
NUM_BUCKETS = 32
MAX_DISTANCE = 1024
IN_W = 2 * BRANCH_W + 3 * QKV_W + BRANCH_W + 2 * D_MODEL
RMS_EPS = 1e-6
NEG_INF = -1e30
LOG2_E = math.log2(math.e)
SCORE_SCALE = LOG2_E / math.sqrt(HEAD_DIM)

COL_Q = 2 * BRANCH_W
COL_K = COL_Q + QKV_W
COL_V = COL_K + QKV_W
COL_ATT_GATE = COL_V + QKV_W
COL_MERGE = COL_ATT_GATE + BRANCH_W

NAT_ATT_GATE = 6
NAT_QKV = 7
NAT_W = 10 * BRANCH_W
QKV_G_W = 3 * BRANCH_W

N_SIDE = 64
LANES = 128
Q_SUB = 128
K_SUB = Q_SUB + 2 * N_SIDE
ATTN_STEP = ((1, 2048), (1, 2048), (4, 512))
HALO_ROWS = 16
POOL_ROWS = 128
POOL_K = POOL_ROWS + 2 * HALO_ROWS

IN_PROJ_ROWS = 512
IN_PROJ_COLS = 512
EPI_ROWS = 512
VMEM_LIMIT = 56 * 1024 * 1024

_BF16 = jnp.bfloat16
_F32 = jnp.float32


def _in_proj_kernel(x_ref, g_ref, w_ref, zn_ref, z4_ref, z16_ref,
                    hf_ref, hn_ref, h4_ref, h16_ref):
    tm = x_ref.shape[0]
    x = x_ref[...]
    r = lax.rsqrt(jnp.mean(x * x, axis=-1, keepdims=True) + RMS_EPS)
    hf = (x * r) * g_ref[...]
    n_slab = D_MODEL // LANES
    for c in range(n_slab):
        hf_ref[c] = hf[:, c * LANES:(c + 1) * LANES]
    hn_ref[...] = hf.astype(_BF16)
    for c in range(NAT_W // IN_PROJ_COLS):
        cs = slice(c * IN_PROJ_COLS, (c + 1) * IN_PROJ_COLS)
        zn_ref[:, cs] = jnp.dot(hn_ref[...], w_ref[:, cs],
                                preferred_element_type=_F32).astype(_BF16)
    for dil, h_ref, col0, z_ref in ((4, h4_ref, NAT_W, z4_ref),
                                    (16, h16_ref, NAT_W + QKV_G_W, z16_ref)):
        rows = tm // dil
        for res in range(dil):
            for c in range(n_slab):
                h_ref[res * rows:(res + 1) * rows, c * LANES:(c + 1) * LANES] = (
                    hf_ref[c, pl.ds(res, rows, stride=dil), :].astype(_BF16))
        for c in range(QKV_G_W // IN_PROJ_COLS):
            cs = slice(c * IN_PROJ_COLS, (c + 1) * IN_PROJ_COLS)
            zc = jnp.dot(h_ref[...], w_ref[:, col0 + cs.start:col0 + cs.stop],
                         preferred_element_type=_F32).astype(_BF16)
            for res in range(dil):
                z_ref[res, :, cs] = zc[res * rows:(res + 1) * rows]


def _in_proj(x, gain, w_all):
    b, s, _ = x.shape
    tm = IN_PROJ_ROWS
    return pl.pallas_call(
        _in_proj_kernel,
        grid=(b, s // tm),
        in_specs=[
            pl.BlockSpec((None, tm, D_MODEL), lambda bi, i: (bi, i, 0)),
            pl.BlockSpec((1, D_MODEL), lambda bi, i: (0, 0)),
            pl.BlockSpec(w_all.shape, lambda bi, i: (0, 0), pipeline_mode=pl.Buffered(1)),
        ],
        out_specs=[
            pl.BlockSpec((None, tm, NAT_W), lambda bi, i: (bi, i, 0)),
            pl.BlockSpec((None, 4, tm // 4, QKV_G_W), lambda bi, i: (bi, 0, i, 0)),
            pl.BlockSpec((None, 16, tm // 16, QKV_G_W), lambda bi, i: (bi, 0, i, 0)),
        ],
        out_shape=[
            jax.ShapeDtypeStruct((b, s, NAT_W), _BF16),
            jax.ShapeDtypeStruct((b, 4, s // 4, QKV_G_W), _BF16),
            jax.ShapeDtypeStruct((b, 16, s // 16, QKV_G_W), _BF16),
        ],
        scratch_shapes=[pltpu.VMEM((D_MODEL // LANES, tm, LANES), _F32),
                        pltpu.VMEM((tm, D_MODEL), _BF16),
                        pltpu.VMEM((tm, D_MODEL), _BF16), pltpu.VMEM((tm, D_MODEL), _BF16)],
        compiler_params=pltpu.CompilerParams(
            dimension_semantics=("arbitrary", "arbitrary"), vmem_limit_bytes=VMEM_LIMIT),
        name="in_proj",
    )(x, gain, w_all)


def _attn_kernel(q_ref, kp_ref, km_ref, kn_ref, vp_ref, vm_ref, vn_ref, bias_ref,
                 o_ref, st_ref, kwin, vwin, *, n_steps):
    n_res, q_step = q_ref.shape[0], q_ref.shape[1]
    j = pl.program_id(2)
    kwin[:, 0:N_SIDE] = kp_ref[...]
    kwin[:, N_SIDE:N_SIDE + q_step] = km_ref[...]
    kwin[:, N_SIDE + q_step:] = kn_ref[...]
    vwin[:, 0:N_SIDE] = vp_ref[...]
    vwin[:, N_SIDE:N_SIDE + q_step] = vm_ref[...]
    vwin[:, N_SIDE + q_step:] = vn_ref[...]

    lane = lax.broadcasted_iota(jnp.int32, (Q_SUB, LANES), 1)
    low_head = lane < HEAD_DIM
    n_sub = q_step // Q_SUB
    ones = jnp.ones((K_SUB, LANES), _BF16)

    for res in range(n_res):
        for i in range(n_sub):
            r0 = i * Q_SUB
            if i == 0:
                var = jnp.where(j == 0, 1, 0)
            elif i == n_sub - 1:
                var = jnp.where(j == n_steps - 1, 2, 0)
            else:
                var = 0
            stats = jnp.zeros((Q_SUB, LANES), _F32)
            for hp in range(HEADS_PER_GROUP // 2):
                cs = slice(hp * LANES, (hp + 1) * LANES)
                q2 = q_ref[res, r0:r0 + Q_SUB, cs]
                k2 = kwin[res, r0:r0 + K_SUB, cs]
                v2 = jnp.concatenate([vwin[res, r0:r0 + K_SUB, cs], ones], axis=1)
                zero = jnp.zeros_like(q2)
                qs = jnp.concatenate([jnp.where(low_head, q2, zero),
                                      jnp.where(low_head, zero, q2)], axis=0)
                s = lax.dot_general(qs, k2, (((1,), (1,)), ((), ())),
                                    preferred_element_type=_F32)
                s = s + bias_ref[var, hp]
                m = jnp.max(s, axis=-1, keepdims=True)
                p = jnp.exp2(s - m)
                ol = jnp.dot(p.astype(_BF16), v2, preferred_element_type=_F32)
                o, l = ol[:, :LANES], ol[:, LANES:]
                o_ref[res, r0:r0 + Q_SUB, cs] = (
                    jnp.where(low_head, o[:Q_SUB], o[Q_SUB:]).astype(_BF16))
                for a in range(2):
                    h = 2 * hp + a
                    rows = slice(a * Q_SUB, (a + 1) * Q_SUB)
                    stats = jnp.where(lane == h, m[rows], stats)
                    stats = jnp.where(lane == HEADS_PER_GROUP + h, l[rows], stats)
            st_ref[res, r0:r0 + Q_SUB, :] = stats


def _attn_group(zg, bias, g, col0, n_res, q_step):
    b, dil, sp, _ = zg.shape
    assert dil % n_res == 0 and sp % q_step == 0 and q_step >= 2 * Q_SUB
    n_steps = sp // q_step
    hb = q_step // N_SIDE
    last_hb = sp // N_SIDE - 1

    def main(col, width=BRANCH_W):
        return pl.BlockSpec((None, n_res, q_step, width), lambda bi, r, j: (bi, r, j, col))

    def prev(col):
        return pl.BlockSpec((None, n_res, N_SIDE, BRANCH_W),
                            lambda bi, r, j: (bi, r, jnp.maximum(j * hb - 1, 0), col))

    def nxt(col):
        return pl.BlockSpec((None, n_res, N_SIDE, BRANCH_W),
                            lambda bi, r, j: (bi, r, jnp.minimum((j + 1) * hb, last_hb), col))

    qc, kc, vc = col0, col0 + 1, col0 + 2
    window = pltpu.VMEM((n_res, q_step + 2 * N_SIDE, BRANCH_W), _BF16)
    return pl.pallas_call(
        functools.partial(_attn_kernel, n_steps=n_steps),
        grid=(b, dil // n_res, n_steps),
        in_specs=[main(qc), prev(kc), main(kc), nxt(kc), prev(vc), main(vc), nxt(vc),
                  pl.BlockSpec((None,) + bias.shape[1:], lambda bi, r, j: (g, 0, 0, 0, 0))],
        out_specs=[main(0), main(0, LANES)],
        out_shape=[
            jax.ShapeDtypeStruct((b, dil, sp, BRANCH_W), _BF16),
            jax.ShapeDtypeStruct((b, dil, sp, LANES), _F32),
        ],
        scratch_shapes=[window, window],
        compiler_params=pltpu.CompilerParams(
            dimension_semantics=("arbitrary", "arbitrary", "arbitrary"),
            vmem_limit_bytes=VMEM_LIMIT),
        name=f"attn_d{dil}",
    )(zg, zg, zg, zg, zg, zg, zg, bias)


def _t5_bucket(rel):
    nb = NUM_BUCKETS // 2
    ret = (rel > 0).astype(jnp.int32) * nb
    n = jnp.abs(rel)
    max_exact = nb // 2
    nf = jnp.maximum(n, 1).astype(_F32)
    large = max_exact + (jnp.log(nf / max_exact) / math.log(MAX_DISTANCE / max_exact)
                         * (nb - max_exact)).astype(jnp.int32)
    large = jnp.minimum(large, nb - 1)
    return ret + jnp.where(n < max_exact, n, large)


def _band_bias(rel_bias):
    n_off = 2 * N_SIDE + 1
    n_heads = N_ATT_GROUPS * HEADS_PER_GROUP
    dils = jnp.asarray([dil for _, dil in DILATED_GROUPS], jnp.int32)[:, None]
    offs = jnp.arange(-N_SIDE, N_SIDE + 1, dtype=jnp.int32)[None, :] * dils
    by_group = rel_bias.reshape(NUM_BUCKETS, N_ATT_GROUPS, HEADS_PER_GROUP)
    per_off = by_group[_t5_bucket(offs), jnp.arange(N_ATT_GROUPS)[:, None]]
    per_off = jnp.swapaxes(per_off, 1, 2).reshape(n_heads, n_off).astype(_F32) * LOG2_E
    ext = jnp.concatenate(
        [per_off, jnp.full((n_heads, K_SUB + 1 - n_off), NEG_INF, _F32)], axis=1)
    band = jnp.tile(ext, (1, Q_SUB))[:, :Q_SUB * K_SUB].reshape(
        N_ATT_GROUPS, 1, HEADS_PER_GROUP, Q_SUB, K_SUB)
    kj = jnp.arange(K_SUB, dtype=jnp.int32)
    first = jnp.where(kj < N_SIDE, NEG_INF, band)
    last = jnp.where(kj >= K_SUB - N_SIDE, NEG_INF, band)
    return jnp.concatenate([band, first, last], axis=1).reshape(
        N_ATT_GROUPS, 3, HEADS_PER_GROUP // 2, 2 * Q_SUB, K_SUB)


def _silu_of_half(h):
    return h + h * jnp.tanh(h)


def _epilogue_kernel(x_ref, za_ref, zprev_ref, znext_ref, zga_ref, zgb_ref, zbg_ref,
                     o1_ref, o4_ref, o16_ref, s1_ref, s4_ref, s16_ref,
                     icount_in, pscale_ref, bgate_ref, fgain_ref,
                     pmat_in, wpool_in, wpa_in, wpb_in, wout_in, expand_in,
                     out_ref,
                     ext_ref, o4n_ref, o16n_ref, s4n_ref, s16n_ref, icount_ref,
                     pmat_ref, wpool_ref, wpa_ref, wpb_ref, wout_ref, expand_ref, *, seq_len):
    tm = x_ref.shape[0]
    p0 = pl.program_id(1) * tm

    @pl.when(jnp.logical_and(pl.program_id(0) == 0, pl.program_id(1) == 0))
    def _stage_weights():
        for src, dst in ((pmat_in, pmat_ref), (wpool_in, wpool_ref), (wpa_in, wpa_ref),
                         (wpb_in, wpb_ref), (wout_in, wout_ref), (expand_in, expand_ref)):
            dst[...] = src[...]
        icount_ref[...] = jnp.broadcast_to(icount_in[0, 0:1, :], icount_ref.shape)

    icount_ref[0:HALO_ROWS] = jnp.where(p0 == 0, icount_in[1], icount_in[0])
    icount_ref[tm - HALO_ROWS:tm] = jnp.where(p0 + tm == seq_len, icount_in[2], icount_in[0])

    has_prev = (p0 > 0).astype(_BF16)
    has_next = (p0 + tm < seq_len).astype(_BF16)
    ext_ref[0:HALO_ROWS] = zprev_ref[...] * has_prev
    ext_ref[HALO_ROWS:HALO_ROWS + tm] = za_ref[:, 0:BRANCH_W]
    ext_ref[HALO_ROWS + tm:] = znext_ref[...] * has_next

    for dil, o_ref, on_ref, s_ref, sn_ref in ((4, o4_ref, o4n_ref, s4_ref, s4n_ref),
                                              (16, o16_ref, o16n_ref, s16_ref, s16n_ref)):
        rows = tm // dil
        for res in range(dil):
            o_res = o_ref[res].astype(_F32)
            for c in range(BRANCH_W // LANES):
                on_ref[c, pl.ds(res, rows, stride=dil), :] = o_res[:, c * LANES:(c + 1) * LANES]
            sn_ref[pl.ds(res, rows, stride=dil), :] = s_ref[res]

    a_in = za_ref[:, 0:BRANCH_W].astype(_F32)
    pooled = []
    for gi in range(len(POOL_WINDOWS)):
        cs = slice(gi * POOL_GROUP_W, (gi + 1) * POOL_GROUP_W)
        win = jnp.concatenate(
            [jnp.dot(pmat_ref[gi], ext_ref[q0:q0 + POOL_K, cs], preferred_element_type=_F32)
             for q0 in range(0, tm, POOL_ROWS)], axis=0)
        pooled.append((win * icount_ref[:, cs] - a_in[:, cs]).astype(_BF16))
    mixed = [jnp.dot(jnp.concatenate(pooled[2 * pr:2 * pr + 2], axis=-1), wpool_ref[pr],
                     preferred_element_type=_F32) for pr in range(len(POOL_WINDOWS) // 2)]
    mixed = jnp.concatenate(mixed, axis=-1) * pscale_ref[...]
    ya_in = mixed * _silu_of_half(za_ref[:, BRANCH_W:].astype(_F32))
    y_a = jnp.dot(ya_in.astype(_BF16), wpa_ref[...], preferred_element_type=_F32)

    st = (s1_ref[...], s4n_ref[...], s16n_ref[...])
    mx = jnp.maximum(jnp.maximum(st[0], st[1]), st[2])
    es = [jnp.exp2(t - mx) for t in st]
    ls = [pltpu.roll(t, LANES - HEADS_PER_GROUP, 1) for t in st]
    head_lane = lax.broadcasted_iota(jnp.int32, (tm, LANES), 1) < HEADS_PER_GROUP
    den = jnp.where(head_lane, es[0] * ls[0] + es[1] * ls[1] + es[2] * ls[2], 1.0)
    inv = 1.0 / den
    packed = jnp.zeros((tm, LANES), _F32)
    for g, e in enumerate(es):
        wgt = jnp.where(head_lane, e * inv, 0.0)
        hi_part = wgt.astype(_BF16).astype(_F32)
        for part, lane0 in ((hi_part, HEADS_PER_GROUP * g),
                            (wgt - hi_part, HEADS_PER_GROUP * (N_ATT_GROUPS + g))):
            packed = packed + (pltpu.roll(part, lane0, 1) if lane0 else part)
    wide = jnp.dot(packed.astype(_BF16), expand_ref[...], preferred_element_type=_F32)
    ogs = (o1_ref[...].astype(_F32),
           jnp.concatenate([o4n_ref[c] for c in range(BRANCH_W // LANES)], axis=-1),
           jnp.concatenate([o16n_ref[c] for c in range(BRANCH_W // LANES)], axis=-1))
    att = sum(wide[:, g * BRANCH_W:(g + 1) * BRANCH_W] * og for g, og in enumerate(ogs))
    yb_in = att * _silu_of_half(zbg_ref[...].astype(_F32))
    y_b = jnp.dot(yb_in.astype(_BF16), wpb_ref[...], preferred_element_type=_F32)

    t_a = jnp.tanh(zga_ref[...].astype(_F32) + bgate_ref[0:1, :])
    t_b = jnp.tanh(zgb_ref[...].astype(_F32) + bgate_ref[1:2, :])
    merged2 = (y_a + y_b) + (t_a * y_a + t_b * y_b)
    xo = x_ref[...] + jnp.dot(merged2.astype(_BF16), wout_ref[...], preferred_element_type=_F32)
    r = lax.rsqrt(jnp.mean(xo * xo, axis=-1, keepdims=True) + RMS_EPS)
    out_ref[...] = (xo * r) * fgain_ref[...]


def _pool_matrices():
    r = np.arange(POOL_ROWS)[:, None]
    c = np.arange(POOL_K)[None, :] - HALO_ROWS
    return np.stack([(c - r >= -(w // 2)) & (c - r <= w // 2 - 1)
                     for w in POOL_WINDOWS]).astype(np.float32)


def _inv_window_counts(seq_len):
    half = np.asarray([w // 2 for w in POOL_WINDOWS])[None, :]
    def rows(pos):
        pos = np.asarray(pos)[:, None]
        count = np.minimum(pos + half - 1, seq_len - 1) - np.maximum(pos - half, 0) + 1
        return np.repeat(1.0 / count, POOL_GROUP_W, axis=1)
    mid = seq_len // 2
    return np.stack([rows(range(mid, mid + HALO_ROWS)), rows(range(HALO_ROWS)),
                     rows(range(seq_len - HALO_ROWS, seq_len))]).astype(np.float32)


def _expand_matrix():
    row = np.arange(LANES)[:, None]
    col = np.arange(N_ATT_GROUPS * BRANCH_W)[None, :]
    n_packed = N_ATT_GROUPS * HEADS_PER_GROUP
    hit = ((row < 2 * n_packed)
           & ((row % n_packed) // HEADS_PER_GROUP == col // BRANCH_W)
           & (row % HEADS_PER_GROUP == (col % BRANCH_W) // HEAD_DIM))
    return hit.astype(np.float32)


def _epilogue(x, z_nat, outs, stats, w_pool, pool_scale, w_pa, w_pb, w_out_half, b_gate_half,
              final_gain):
    b, s, _ = x.shape
    tm = EPI_ROWS
    halo_per_tile = tm // HALO_ROWS
    last_halo = s // HALO_ROWS - 1
    staged = (jnp.asarray(_pool_matrices(), _BF16), w_pool, w_pa, w_pb, w_out_half,
              jnp.asarray(_expand_matrix(), _BF16))

    def full(a):
        return pl.BlockSpec(a.shape, lambda bi, i: (0,) * a.ndim)

    def once(a):
        return pl.BlockSpec(a.shape, lambda bi, i: (0,) * a.ndim, pipeline_mode=pl.Buffered(1))

    def row(w, col):
        return pl.BlockSpec((None, tm, w), lambda bi, i: (bi, i, col))

    def by_residue(a):
        dil, w = a.shape[1], a.shape[3]
        return pl.BlockSpec((None, dil, tm // dil, w), lambda bi, i: (bi, 0, i, 0))

    in_specs = [
        row(D_MODEL, 0),
        row(2 * BRANCH_W, 0),
        pl.BlockSpec((None, HALO_ROWS, BRANCH_W),
                     lambda bi, i: (bi, jnp.maximum(i * halo_per_tile - 1, 0), 0)),
        pl.BlockSpec((None, HALO_ROWS, BRANCH_W),
                     lambda bi, i: (bi, jnp.minimum((i + 1) * halo_per_tile, last_halo), 0)),
        row(D_MODEL, 1), row(D_MODEL, 2),
        row(BRANCH_W, NAT_ATT_GATE),
        pl.BlockSpec((None, None, tm, BRANCH_W), lambda bi, i: (bi, 0, i, 0)),
        by_residue(outs[1]), by_residue(outs[2]),
        pl.BlockSpec((None, None, tm, LANES), lambda bi, i: (bi, 0, i, 0)),
        by_residue(stats[1]), by_residue(stats[2]),
        pl.BlockSpec((3, HALO_ROWS, BRANCH_W), lambda bi, i: (0, 0, 0)),
        full(pool_scale), full(b_gate_half), full(final_gain),
    ] + [once(a) for a in staged]
    return pl.pallas_call(
        functools.partial(_epilogue_kernel, seq_len=s),
        grid=(b, s // tm),
        in_specs=in_specs,
        out_specs=pl.BlockSpec((None, tm, D_MODEL), lambda bi, i: (bi, i, 0)),
        out_shape=jax.ShapeDtypeStruct((b, s, D_MODEL), _F32),
        scratch_shapes=[pltpu.VMEM((tm + 2 * HALO_ROWS, BRANCH_W), _BF16),
                        pltpu.VMEM((BRANCH_W // LANES, tm, LANES), _F32),
                        pltpu.VMEM((BRANCH_W // LANES, tm, LANES), _F32),
                        pltpu.VMEM((tm, LANES), _F32), pltpu.VMEM((tm, LANES), _F32),
                        pltpu.VMEM((tm, BRANCH_W), _F32)]
        + [pltpu.VMEM(a.shape, _BF16) for a in staged],
        compiler_params=pltpu.CompilerParams(
            dimension_semantics=("arbitrary", "arbitrary"), vmem_limit_bytes=VMEM_LIMIT),
        name="epilogue",
    )(x, z_nat, z_nat, z_nat, z_nat, z_nat, z_nat, *outs, *stats,
      jnp.asarray(_inv_window_counts(s)), pool_scale, b_gate_half, final_gain, *staged)


def _split_w_in(w):
    def qkv(g):
        q, k, v = (w[:, c + g * BRANCH_W:c + (g + 1) * BRANCH_W] for c in (COL_Q, COL_K, COL_V))
        return [q * SCORE_SCALE, k, v]
    nat = [w[:, :BRANCH_W], 0.5 * w[:, BRANCH_W:COL_Q], 0.5 * w[:, COL_MERGE:],
           0.5 * w[:, COL_ATT_GATE:COL_MERGE]] + qkv(0)
    return jnp.concatenate(nat + qkv(1) + qkv(2), axis=1).astype(_BF16)


def _pair_block_diag(w_pool):
    zero = jnp.zeros_like(w_pool[0])
    pair = lambda a, b: jnp.block([[a, zero], [zero, b]])
    return jnp.stack([pair(w_pool[0], w_pool[1]), pair(w_pool[2], w_pool[3])]).astype(_BF16)


def kernel(x, norm_gain, w_in, b_gate, rel_bias, w_pool, pool_scale, w_proj_a, w_proj_b,
           w_out, final_gain):
    b, s, d = x.shape
    assert d == D_MODEL and norm_gain.shape[0] == 1, "single-layer block of width D_MODEL"
    assert s % EPI_ROWS == 0 and s % IN_PROJ_ROWS == 0
    assert [dil for _, dil in DILATED_GROUPS] == [1, 4, 16]
    z_nat, z_d4, z_d16 = _in_proj(x, norm_gain[0][None, :], _split_w_in(w_in[0]))
    bias = _band_bias(rel_bias)
    outs, stats = [], []
    for g, (zg, col0) in enumerate(((z_nat[:, None], NAT_QKV), (z_d4, 0), (z_d16, 0))):
        o, st = _attn_group(zg, bias, g, col0, *ATTN_STEP[g])
        outs.append(o)
        stats.append(st)
    return _epilogue(x, z_nat, outs, stats, _pair_block_diag(w_pool[0]), pool_scale[0][None, :],
                     w_proj_a[0].astype(_BF16), w_proj_b[0].astype(_BF16),
                     (0.5 * w_out[0]).astype(_BF16), 0.5 * b_gate[0], final_gain[None, :])
```

```python
import functools
import math

import jax
import jax.numpy as jnp
import numpy as np
from jax import lax
from jax.experimental import pallas as pl
from jax.experimental.pallas import tpu as pltpu

D_MODEL = 1024
BRANCH_W = D_MODEL // 2
POOL_WINDOWS = (2, 4, 8, 16)
POOL_GROUP_W = BRANCH_W // len(POOL_WINDOWS)
DILATED_GROUPS = ((128, 1), (512, 4), (2048, 16))
N_ATT_GROUPS = len(DILATED_GROUPS)
HEAD_DIM = 64
HEADS_PER_GROUP = BRANCH_W // HEAD_DIM
QKV_W = N_ATT_GROUPS * BRANCH_W
NUM_BUCKETS = 32
MAX_DISTANCE = 1024
IN_W = 2 * BRANCH_W + 3 * QKV_W + BRANCH_W + 2 * D_MODEL
RMS_EPS = 1e-6
NEG_INF = -1e30
LOG2_E = math.log2(math.e)
SCORE_SCALE = LOG2_E / math.sqrt(HEAD_DIM)

COL_Q = 2 * BRANCH_W
COL_K = COL_Q + QKV_W
COL_V = COL_K + QKV_W
COL_ATT_GATE = COL_V + QKV_W
COL_MERGE = COL_ATT_GATE + BRANCH_W

NAT_ATT_GATE = 6
NAT_QKV = 7
NAT_W = 10 * BRANCH_W
QKV_G_W = 3 * BRANCH_W

N_SIDE = 64
LANES = 128
Q_SUB = 128
K_SUB = Q_SUB + 2 * N_SIDE
ATTN_STEP = ((1, 2048), (1, 2048), (4, 512))
HALO_ROWS = 16
POOL_ROWS = 128
POOL_K = POOL_ROWS + 2 * HALO_ROWS

IN_PROJ_ROWS = 512
IN_PROJ_COLS = 512
EPI_ROWS = 512
V7X_VMEM_BYTES = 64 * 1024 * 1024
VMEM_LIMIT = V7X_VMEM_BYTES * 7 // 8

_BF16 = jnp.bfloat16
_F32 = jnp.float32


def _in_proj_kernel(x_ref, g_ref, w_ref, zn_ref, z4_ref, z16_ref,
                    hf_ref, hn_ref, h4_ref, h16_ref):
    tm = x_ref.shape[0]
    x = x_ref[...]
    r = lax.rsqrt(jnp.mean(x * x, axis=-1, keepdims=True) + RMS_EPS)
    hf = (x * r) * g_ref[...]
    n_slab = D_MODEL // LANES
    for c in range(n_slab):
        hf_ref[c] = hf[:, c * LANES:(c + 1) * LANES]
    hn_ref[...] = hf.astype(_BF16)
    for c in range(NAT_W // IN_PROJ_COLS):
        cs = slice(c * IN_PROJ_COLS, (c + 1) * IN_PROJ_COLS)
        zn_ref[:, cs] = jnp.dot(hn_ref[...], w_ref[:, cs],
                                preferred_element_type=_F32).astype(_BF16)
    for dil, h_ref, col0, z_ref in ((4, h4_ref, NAT_W, z4_ref),
                                    (16, h16_ref, NAT_W + QKV_G_W, z16_ref)):
        rows = tm // dil
        for res in range(dil):
            for c in range(n_slab):
                h_ref[res * rows:(res + 1) * rows, c * LANES:(c + 1) * LANES] = (
                    hf_ref[c, pl.ds(res, rows, stride=dil), :].astype(_BF16))
        for c in range(QKV_G_W // IN_PROJ_COLS):
            cs = slice(c * IN_PROJ_COLS, (c + 1) * IN_PROJ_COLS)
            zc = jnp.dot(h_ref[...], w_ref[:, col0 + cs.start:col0 + cs.stop],
                         preferred_element_type=_F32).astype(_BF16)
            for res in range(dil):
                z_ref[res, :, cs] = zc[res * rows:(res + 1) * rows]


def _in_proj(x, gain, w_all):
    b, s, _ = x.shape
    tm = IN_PROJ_ROWS
    return pl.pallas_call(
        _in_proj_kernel,
        grid=(b, s // tm),
        in_specs=[
            pl.BlockSpec((None, tm, D_MODEL), lambda bi, i: (bi, i, 0)),
            pl.BlockSpec((1, D_MODEL), lambda bi, i: (0, 0)),
            pl.BlockSpec(w_all.shape, lambda bi, i: (0, 0), pipeline_mode=pl.Buffered(1)),
        ],
        out_specs=[
            pl.BlockSpec((None, tm, NAT_W), lambda bi, i: (bi, i, 0)),
            pl.BlockSpec((None, 4, tm // 4, QKV_G_W), lambda bi, i: (bi, 0, i, 0)),
            pl.BlockSpec((None, 16, tm // 16, QKV_G_W), lambda bi, i: (bi, 0, i, 0)),
        ],
        out_shape=[
            jax.ShapeDtypeStruct((b, s, NAT_W), _BF16),
            jax.ShapeDtypeStruct((b, 4, s // 4, QKV_G_W), _BF16),
            jax.ShapeDtypeStruct((b, 16, s // 16, QKV_G_W), _BF16),
        ],
        scratch_shapes=[pltpu.VMEM((D_MODEL // LANES, tm, LANES), _F32),
                        pltpu.VMEM((tm, D_MODEL), _BF16),
                        pltpu.VMEM((tm, D_MODEL), _BF16), pltpu.VMEM((tm, D_MODEL), _BF16)],
        compiler_params=pltpu.CompilerParams(
            dimension_semantics=("arbitrary", "arbitrary"), vmem_limit_bytes=VMEM_LIMIT),
        name="in_proj",
    )(x, gain, w_all)


def _attn_kernel(q_ref, kp_ref, km_ref, kn_ref, vp_ref, vm_ref, vn_ref, bias_ref,
                 o_ref, st_ref, kwin, vwin, *, n_steps):
    n_res, q_step = q_ref.shape[0], q_ref.shape[1]
    j = pl.program_id(2)
    kwin[:, 0:N_SIDE] = kp_ref[...]
    kwin[:, N_SIDE:N_SIDE + q_step] = km_ref[...]
    kwin[:, N_SIDE + q_step:] = kn_ref[...]
    vwin[:, 0:N_SIDE] = vp_ref[...]
    vwin[:, N_SIDE:N_SIDE + q_step] = vm_ref[...]
    vwin[:, N_SIDE + q_step:] = vn_ref[...]

    lane = lax.broadcasted_iota(jnp.int32, (Q_SUB, LANES), 1)
    low_head = lane < HEAD_DIM
    n_sub = q_step // Q_SUB
    ones = jnp.ones((K_SUB, LANES), _BF16)

    for res in range(n_res):
        for i in range(n_sub):
            r0 = i * Q_SUB
            if i == 0:
                var = jnp.where(j == 0, 1, 0)
            elif i == n_sub - 1:
                var = jnp.where(j == n_steps - 1, 2, 0)
            else:
                var = 0
            stats = jnp.zeros((Q_SUB, LANES), _F32)
            for hp in range(HEADS_PER_GROUP // 2):
                cs = slice(hp * LANES, (hp + 1) * LANES)
                q2 = q_ref[res, r0:r0 + Q_SUB, cs]
                k2 = kwin[res, r0:r0 + K_SUB, cs]
                v2 = jnp.concatenate([vwin[res, r0:r0 + K_SUB, cs], ones], axis=1)
                zero = jnp.zeros_like(q2)
                qs = jnp.concatenate([jnp.where(low_head, q2, zero),
                                      jnp.where(low_head, zero, q2)], axis=0)
                s = lax.dot_general(qs, k2, (((1,), (1,)), ((), ())),
                                    preferred_element_type=_F32)
                s = s + bias_ref[var, hp]
                m = jnp.max(s, axis=-1, keepdims=True)
                p = jnp.exp2(s - m)
                ol = jnp.dot(p.astype(_BF16), v2, preferred_element_type=_F32)
                o, l = ol[:, :LANES], ol[:, LANES:]
                o_ref[res, r0:r0 + Q_SUB, cs] = (
                    jnp.where(low_head, o[:Q_SUB], o[Q_SUB:]).astype(_BF16))
                for a in range(2):
                    h = 2 * hp + a
                    rows = slice(a * Q_SUB, (a + 1) * Q_SUB)
                    stats = jnp.where(lane == h, m[rows], stats)
                    stats = jnp.where(lane == HEADS_PER_GROUP + h, l[rows], stats)
            st_ref[res, r0:r0 + Q_SUB, :] = stats


def _attn_group(zg, bias, g, col0, n_res, q_step):
    b, dil, sp, _ = zg.shape
    assert dil % n_res == 0 and sp % q_step == 0 and q_step >= 2 * Q_SUB
    n_steps = sp // q_step
    hb = q_step // N_SIDE
    last_hb = sp // N_SIDE - 1

    def main(col, width=BRANCH_W):
        return pl.BlockSpec((None, n_res, q_step, width), lambda bi, r, j: (bi, r, j, col))

    def prev(col):
        return pl.BlockSpec((None, n_res, N_SIDE, BRANCH_W),
                            lambda bi, r, j: (bi, r, jnp.maximum(j * hb - 1, 0), col))

    def nxt(col):
        return pl.BlockSpec((None, n_res, N_SIDE, BRANCH_W),
                            lambda bi, r, j: (bi, r, jnp.minimum((j + 1) * hb, last_hb), col))

    qc, kc, vc = col0, col0 + 1, col0 + 2
    window = pltpu.VMEM((n_res, q_step + 2 * N_SIDE, BRANCH_W), _BF16)
    return pl.pallas_call(
        functools.partial(_attn_kernel, n_steps=n_steps),
        grid=(b, dil // n_res, n_steps),
        in_specs=[main(qc), prev(kc), main(kc), nxt(kc), prev(vc), main(vc), nxt(vc),
                  pl.BlockSpec((None,) + bias.shape[1:], lambda bi, r, j: (g, 0, 0, 0, 0))],
        out_specs=[main(0), main(0, LANES)],
        out_shape=[
            jax.ShapeDtypeStruct((b, dil, sp, BRANCH_W), _BF16),
            jax.ShapeDtypeStruct((b, dil, sp, LANES), _F32),
        ],
        scratch_shapes=[window, window],
        compiler_params=pltpu.CompilerParams(
            dimension_semantics=("arbitrary", "arbitrary", "arbitrary"),
            vmem_limit_bytes=VMEM_LIMIT),
        name=f"attn_d{dil}",
    )(zg, zg, zg, zg, zg, zg, zg, bias)


def _t5_bucket(rel):
    nb = NUM_BUCKETS // 2
    ret = (rel > 0).astype(jnp.int32) * nb
    n = jnp.abs(rel)
    max_exact = nb // 2
    nf = jnp.maximum(n, 1).astype(_F32)
    large = max_exact + (jnp.log(nf / max_exact) / math.log(MAX_DISTANCE / max_exact)
                         * (nb - max_exact)).astype(jnp.int32)
    large = jnp.minimum(large, nb - 1)
    return ret + jnp.where(n < max_exact, n, large)


def _band_bias(rel_bias):
    n_off = 2 * N_SIDE + 1
    n_heads = N_ATT_GROUPS * HEADS_PER_GROUP
    dils = jnp.asarray([dil for _, dil in DILATED_GROUPS], jnp.int32)[:, None]
    offs = jnp.arange(-N_SIDE, N_SIDE + 1, dtype=jnp.int32)[None, :] * dils
    by_group = rel_bias.reshape(NUM_BUCKETS, N_ATT_GROUPS, HEADS_PER_GROUP)
    per_off = by_group[_t5_bucket(offs), jnp.arange(N_ATT_GROUPS)[:, None]]
    per_off = jnp.swapaxes(per_off, 1, 2).reshape(n_heads, n_off).astype(_F32) * LOG2_E
    ext = jnp.concatenate(
        [per_off, jnp.full((n_heads, K_SUB + 1 - n_off), NEG_INF, _F32)], axis=1)
    band = jnp.tile(ext, (1, Q_SUB))[:, :Q_SUB * K_SUB].reshape(
        N_ATT_GROUPS, 1, HEADS_PER_GROUP, Q_SUB, K_SUB)
    kj = jnp.arange(K_SUB, dtype=jnp.int32)
    first = jnp.where(kj < N_SIDE, NEG_INF, band)
    last = jnp.where(kj >= K_SUB - N_SIDE, NEG_INF, band)
    return jnp.concatenate([band, first, last], axis=1).reshape(
        N_ATT_GROUPS, 3, HEADS_PER_GROUP // 2, 2 * Q_SUB, K_SUB)


def _silu_of_half(h):
    return h + h * jnp.tanh(h)


def _epilogue_kernel(x_ref, za_ref, zprev_ref, znext_ref, zga_ref, zgb_ref, zbg_ref,
                     o1_ref, o4_ref, o16_ref, s1_ref, s4_ref, s16_ref,
                     icount_in, pscale_ref, bgate_ref, fgain_ref,
                     pmat_in, wpool_in, wpa_in, wpb_in, wout_in, expand_in,
                     out_ref,
                     ext_ref, o4n_ref, o16n_ref, s4n_ref, s16n_ref, icount_ref,
                     pmat_ref, wpool_ref, wpa_ref, wpb_ref, wout_ref, expand_ref, *, seq_len):
    tm = x_ref.shape[0]
    p0 = pl.program_id(1) * tm

    @pl.when(jnp.logical_and(pl.program_id(0) == 0, pl.program_id(1) == 0))
    def _stage_weights():
        for src, dst in ((pmat_in, pmat_ref), (wpa_in, wpa_ref), (wpb_in, wpb_ref),
                         (expand_in, expand_ref)):
            dst[...] = src[...].astype(_BF16)
        wout_ref[...] = (0.5 * wout_in[...]).astype(_BF16)
        wpool_ref[...] = jnp.zeros_like(wpool_ref)
        for gi in range(len(POOL_WINDOWS)):
            d0 = (gi % 2) * POOL_GROUP_W
            wpool_ref[gi // 2, d0:d0 + POOL_GROUP_W, d0:d0 + POOL_GROUP_W] = (
                wpool_in[gi].astype(_BF16))
        icount_ref[...] = jnp.broadcast_to(icount_in[0, 0:1, :], icount_ref.shape)

    icount_ref[0:HALO_ROWS] = jnp.where(p0 == 0, icount_in[1], icount_in[0])
    icount_ref[tm - HALO_ROWS:tm] = jnp.where(p0 + tm == seq_len, icount_in[2], icount_in[0])

    has_prev = (p0 > 0).astype(_BF16)
    has_next = (p0 + tm < seq_len).astype(_BF16)
    ext_ref[0:HALO_ROWS] = zprev_ref[...] * has_prev
    ext_ref[HALO_ROWS:HALO_ROWS + tm] = za_ref[:, 0:BRANCH_W]
    ext_ref[HALO_ROWS + tm:] = znext_ref[...] * has_next

    for dil, o_ref, on_ref, s_ref, sn_ref in ((4, o4_ref, o4n_ref, s4_ref, s4n_ref),
                                              (16, o16_ref, o16n_ref, s16_ref, s16n_ref)):
        rows = tm // dil
        for res in range(dil):
            o_res = o_ref[res].astype(_F32)
            for c in range(BRANCH_W // LANES):
                on_ref[c, pl.ds(res, rows, stride=dil), :] = o_res[:, c * LANES:(c + 1) * LANES]
            sn_ref[pl.ds(res, rows, stride=dil), :] = s_ref[res]

    n_chunks = tm // POOL_ROWS
    edge = [jnp.where(p0 == 0, 1, 0)] + [0] * (n_chunks - 2) + [jnp.where(p0 + tm == seq_len, 2, 0)]
    pooled = []
    for gi in range(len(POOL_WINDOWS)):
        cs = slice(gi * POOL_GROUP_W, (gi + 1) * POOL_GROUP_W)
        win = jnp.concatenate(
            [jnp.dot(pmat_ref[edge[ci], gi], ext_ref[ci * POOL_ROWS:ci * POOL_ROWS + POOL_K, cs],
                     preferred_element_type=_F32) for ci in range(n_chunks)], axis=0)
        pooled.append((win * icount_ref[:, cs]).astype(_BF16))
    mixed = [jnp.dot(jnp.concatenate(pooled[2 * pr:2 * pr + 2], axis=-1), wpool_ref[pr],
                     preferred_element_type=_F32) for pr in range(len(POOL_WINDOWS) // 2)]
    mixed = jnp.concatenate(mixed, axis=-1) * pscale_ref[...]
    ya_in = mixed * _silu_of_half(za_ref[:, BRANCH_W:].astype(_F32))
    y_a = jnp.dot(ya_in.astype(_BF16), wpa_ref[...], preferred_element_type=_F32)

    st = (s1_ref[...], s4n_ref[...], s16n_ref[...])
    mx = jnp.maximum(jnp.maximum(st[0], st[1]), st[2])
    es = [jnp.exp2(t - mx) for t in st]
    ls = [pltpu.roll(t, LANES - HEADS_PER_GROUP, 1) for t in st]
    head_lane = lax.broadcasted_iota(jnp.int32, (tm, LANES), 1) < HEADS_PER_GROUP
    den = jnp.where(head_lane, es[0] * ls[0] + es[1] * ls[1] + es[2] * ls[2], 1.0)
    inv = 1.0 / den
    packed = jnp.zeros((tm, LANES), _F32)
    for g, e in enumerate(es):
        wgt = jnp.where(head_lane, e * inv, 0.0)
        hi_part = wgt.astype(_BF16).astype(_F32)
        for part, lane0 in ((hi_part, HEADS_PER_GROUP * g),
                            (wgt - hi_part, HEADS_PER_GROUP * (N_ATT_GROUPS + g))):
            packed = packed + (pltpu.roll(part, lane0, 1) if lane0 else part)
    wide = jnp.dot(packed.astype(_BF16), expand_ref[...], preferred_element_type=_F32)
    ogs = (o1_ref[...].astype(_F32),
           jnp.concatenate([o4n_ref[c] for c in range(BRANCH_W // LANES)], axis=-1),
           jnp.concatenate([o16n_ref[c] for c in range(BRANCH_W // LANES)], axis=-1))
    att = sum(wide[:, g * BRANCH_W:(g + 1) * BRANCH_W] * og for g, og in enumerate(ogs))
    yb_in = att * _silu_of_half(zbg_ref[...].astype(_F32))
    y_b = jnp.dot(yb_in.astype(_BF16), wpb_ref[...], preferred_element_type=_F32)

    t_a = jnp.tanh(zga_ref[...].astype(_F32) + bgate_ref[0:1, :])
    t_b = jnp.tanh(zgb_ref[...].astype(_F32) + bgate_ref[1:2, :])
    merged2 = (y_a + y_b) + (t_a * y_a + t_b * y_b)
    xo = x_ref[...] + jnp.dot(merged2.astype(_BF16), wout_ref[...], preferred_element_type=_F32)
    r = lax.rsqrt(jnp.mean(xo * xo, axis=-1, keepdims=True) + RMS_EPS)
    out_ref[...] = (xo * r) * fgain_ref[...]


def _window_counts(seq_len, pos):
    pos = np.asarray(pos)[:, None]
    half = np.asarray([w // 2 for w in POOL_WINDOWS])[None, :]
    return np.minimum(pos + half - 1, seq_len - 1) - np.maximum(pos - half, 0) + 1


def _pool_matrices(seq_len):
    r = np.arange(POOL_ROWS)[:, None]
    c = np.arange(POOL_K)[None, :] - HALO_ROWS
    mid = seq_len // 2
    starts = (mid, 0, seq_len - POOL_ROWS)
    out = np.zeros((3, len(POOL_WINDOWS), POOL_ROWS, POOL_K), np.float32)
    for v, start in enumerate(starts):
        counts = _window_counts(seq_len, range(start, start + POOL_ROWS))
        for gi, w in enumerate(POOL_WINDOWS):
            window = (c - r >= -(w // 2)) & (c - r <= w // 2 - 1)
            out[v, gi] = window - counts[:, gi:gi + 1] * (c == r)
    return out


def _inv_window_counts(seq_len):
    mid = seq_len // 2
    rows = lambda start: np.repeat(
        1.0 / _window_counts(seq_len, range(start, start + HALO_ROWS)), POOL_GROUP_W, axis=1)
    return np.stack([rows(mid), rows(0), rows(seq_len - HALO_ROWS)]).astype(np.float32)


def _expand_matrix():
    row = np.arange(LANES)[:, None]
    col = np.arange(N_ATT_GROUPS * BRANCH_W)[None, :]
    n_packed = N_ATT_GROUPS * HEADS_PER_GROUP
    hit = ((row < 2 * n_packed)
           & ((row % n_packed) // HEADS_PER_GROUP == col // BRANCH_W)
           & (row % HEADS_PER_GROUP == (col % BRANCH_W) // HEAD_DIM))
    return hit.astype(np.float32)


def _epilogue(x, z_nat, outs, stats, w_pool, pool_scale, w_pa, w_pb, w_out, b_gate_half,
              final_gain):
    b, s, _ = x.shape
    tm = EPI_ROWS
    halo_per_tile = tm // HALO_ROWS
    last_halo = s // HALO_ROWS - 1
    pair_w = 2 * POOL_GROUP_W
    staged = ((jnp.asarray(_pool_matrices(s), _BF16), None),
              (w_pool, (len(POOL_WINDOWS) // 2, pair_w, pair_w)),
              (w_pa, None), (w_pb, None), (w_out, None),
              (jnp.asarray(_expand_matrix(), _BF16), None))

    def full(a):
        return pl.BlockSpec(a.shape, lambda bi, i: (0,) * a.ndim)

    def once(a):
        return pl.BlockSpec(a.shape, lambda bi, i: (0,) * a.ndim, pipeline_mode=pl.Buffered(1))

    def row(w, col):
        return pl.BlockSpec((None, tm, w), lambda bi, i: (bi, i, col))

    def by_residue(a):
        dil, w = a.shape[1], a.shape[3]
        return pl.BlockSpec((None, dil, tm // dil, w), lambda bi, i: (bi, 0, i, 0))

    in_specs = [
        row(D_MODEL, 0),
        row(2 * BRANCH_W, 0),
        pl.BlockSpec((None, HALO_ROWS, BRANCH_W),
                     lambda bi, i: (bi, jnp.maximum(i * halo_per_tile - 1, 0), 0)),
        pl.BlockSpec((None, HALO_ROWS, BRANCH_W),
                     lambda bi, i: (bi, jnp.minimum((i + 1) * halo_per_tile, last_halo), 0)),
        row(D_MODEL, 1), row(D_MODEL, 2),
        row(BRANCH_W, NAT_ATT_GATE),
        pl.BlockSpec((None, None, tm, BRANCH_W), lambda bi, i: (bi, 0, i, 0)),
        by_residue(outs[1]), by_residue(outs[2]),
        pl.BlockSpec((None, None, tm, LANES), lambda bi, i: (bi, 0, i, 0)),
        by_residue(stats[1]), by_residue(stats[2]),
        pl.BlockSpec((3, HALO_ROWS, BRANCH_W), lambda bi, i: (0, 0, 0)),
        full(pool_scale), full(b_gate_half), full(final_gain),
    ] + [once(a) for a, _ in staged]
    return pl.pallas_call(
        functools.partial(_epilogue_kernel, seq_len=s),
        grid=(b, s // tm),
        in_specs=in_specs,
        out_specs=pl.BlockSpec((None, tm, D_MODEL), lambda bi, i: (bi, i, 0)),
        out_shape=jax.ShapeDtypeStruct((b, s, D_MODEL), _F32),
        scratch_shapes=[pltpu.VMEM((tm + 2 * HALO_ROWS, BRANCH_W), _BF16),
                        pltpu.VMEM((BRANCH_W // LANES, tm, LANES), _F32),
                        pltpu.VMEM((BRANCH_W // LANES, tm, LANES), _F32),
                        pltpu.VMEM((tm, LANES), _F32), pltpu.VMEM((tm, LANES), _F32),
                        pltpu.VMEM((tm, BRANCH_W), _F32)]
        + [pltpu.VMEM(shape or a.shape, _BF16) for a, shape in staged],
        compiler_params=pltpu.CompilerParams(
            dimension_semantics=("arbitrary", "arbitrary"), vmem_limit_bytes=VMEM_LIMIT),
        name="epilogue",
    )(x, z_nat, z_nat, z_nat, z_nat, z_nat, z_nat, *outs, *stats,
      jnp.asarray(_inv_window_counts(s)), pool_scale, b_gate_half, final_gain,
      *[a for a, _ in staged])


def _split_w_in(w):
    def qkv(g):
        q, k, v = (w[:, c + g * BRANCH_W:c + (g + 1) * BRANCH_W] for c in (COL_Q, COL_K, COL_V))
        return [q * SCORE_SCALE, k, v]
    nat = [w[:, :BRANCH_W], 0.5 * w[:, BRANCH_W:COL_Q], 0.5 * w[:, COL_MERGE:],
           0.5 * w[:, COL_ATT_GATE:COL_MERGE]] + qkv(0)
    return jnp.concatenate(nat + qkv(1) + qkv(2), axis=1).astype(_BF16)


def kernel(x, norm_gain, w_in, b_gate, rel_bias, w_pool, pool_scale, w_proj_a, w_proj_b,
           w_out, final_gain):
    b, s, d = x.shape
    assert d == D_MODEL and norm_gain.shape[0] == 1, "single-layer block of width D_MODEL"
    assert s % EPI_ROWS == 0 and s % IN_PROJ_ROWS == 0
    assert [dil for _, dil in DILATED_GROUPS] == [1, 4, 16]
    z_nat, z_d4, z_d16 = _in_proj(x, norm_gain[0][None, :], _split_w_in(w_in[0]))
    bias = _band_bias(rel_bias)
    outs, stats = [], []
    for g, (zg, col0) in enumerate(((z_nat[:, None], NAT_QKV), (z_d4, 0), (z_d16, 0))):
        o, st = _attn_group(zg, bias, g, col0, *ATTN_STEP[g])
        outs.append(o)
        stats.append(st)
    return _epilogue(x, z_nat, outs, stats, w_pool[0], pool_scale[0][None, :],
                     w_proj_a[0], w_proj_b[0], w_out[0], 0.5 * b_gate[0], final_gain[None, :])
```

```python
import functools
import math

import jax
import jax.numpy as jnp
import numpy as np
from jax import lax
from jax.experimental import pallas as pl
from jax.experimental.pallas import tpu as pltpu

D_MODEL = 1024
BRANCH_W = D_MODEL // 2
POOL_WINDOWS = (2, 4, 8, 16)
POOL_GROUP_W = BRANCH_W // len(POOL_WINDOWS)
DILATED_GROUPS = ((128, 1), (512, 4), (2048, 16))
N_ATT_GROUPS = len(DILATED_GROUPS)
HEAD_DIM = 64
HEADS_PER_GROUP = BRANCH_W // HEAD_DIM
QKV_W = N_ATT_GROUPS * BRANCH_W
NUM_BUCKETS = 32
MAX_DISTANCE = 1024
IN_W = 2 * BRANCH_W + 3 * QKV_W + BRANCH_W + 2 * D_MODEL
RMS_EPS = 1e-6
NEG_INF = -1e30
LOG2_E = math.log2(math.e)
SCORE_SCALE = LOG2_E / math.sqrt(HEAD_DIM)

COL_Q = 2 * BRANCH_W
COL_K = COL_Q + QKV_W
COL_V = COL_K + QKV_W
COL_ATT_GATE = COL_V + QKV_W
COL_MERGE = COL_ATT_GATE + BRANCH_W

NAT_ATT_GATE = 6
NAT_QKV = 7
NAT_W = 10 * BRANCH_W
QKV_G_W = 3 * BRANCH_W

N_SIDE = 64
LANES = 128
Q_SUB = 128
K_SUB = Q_SUB + 2 * N_SIDE
ATTN_STEP = ((1, 4096), (2, 2048), (8, 512))
HALO_ROWS = 16
POOL_ROWS = 128
POOL_K = POOL_ROWS + 2 * HALO_ROWS

IN_PROJ_ROWS = 512
IN_PROJ_COLS = 512
EPI_ROWS = 512
V7X_VMEM_BYTES = 64 * 1024 * 1024
VMEM_LIMIT = V7X_VMEM_BYTES * 7 // 8

_BF16 = jnp.bfloat16
_F32 = jnp.float32


def _in_proj_kernel(x_ref, g_ref, w_ref, zn_ref, z4_ref, z16_ref,
                    hf_ref, hn_ref, h4_ref, h16_ref):
    tm = x_ref.shape[0]
    x = x_ref[...]
    r = lax.rsqrt(jnp.mean(x * x, axis=-1, keepdims=True) + RMS_EPS)
    hf = (x * r) * g_ref[...]
    n_slab = D_MODEL // LANES
    for c in range(n_slab):
        hf_ref[c] = hf[:, c * LANES:(c + 1) * LANES]
    hn_ref[...] = hf.astype(_BF16)
    for c in range(NAT_W // IN_PROJ_COLS):
        cs = slice(c * IN_PROJ_COLS, (c + 1) * IN_PROJ_COLS)
        zn_ref[:, cs] = jnp.dot(hn_ref[...], w_ref[:, cs],
                                preferred_element_type=_F32).astype(_BF16)
    for dil, h_ref, col0, z_ref in ((4, h4_ref, NAT_W, z4_ref),
                                    (16, h16_ref, NAT_W + QKV_G_W, z16_ref)):
        rows = tm // dil
        for res in range(dil):
            for c in range(n_slab):
                h_ref[res * rows:(res + 1) * rows, c * LANES:(c + 1) * LANES] = (
                    hf_ref[c, pl.ds(res, rows, stride=dil), :].astype(_BF16))
        for c in range(QKV_G_W // IN_PROJ_COLS):
            cs = slice(c * IN_PROJ_COLS, (c + 1) * IN_PROJ_COLS)
            zc = jnp.dot(h_ref[...], w_ref[:, col0 + cs.start:col0 + cs.stop],
                         preferred_element_type=_F32).astype(_BF16)
            for res in range(dil):
                z_ref[res, :, cs] = zc[res * rows:(res + 1) * rows]


def _in_proj(x, gain, w_all):
    b, s, _ = x.shape
    tm = IN_PROJ_ROWS
    return pl.pallas_call(
        _in_proj_kernel,
        grid=(b, s // tm),
        in_specs=[
            pl.BlockSpec((None, tm, D_MODEL), lambda bi, i: (bi, i, 0)),
            pl.BlockSpec((1, D_MODEL), lambda bi, i: (0, 0)),
            pl.BlockSpec(w_all.shape, lambda bi, i: (0, 0), pipeline_mode=pl.Buffered(1)),
        ],
        out_specs=[
            pl.BlockSpec((None, tm, NAT_W), lambda bi, i: (bi, i, 0)),
            pl.BlockSpec((None, 4, tm // 4, QKV_G_W), lambda bi, i: (bi, 0, i, 0)),
            pl.BlockSpec((None, 16, tm // 16, QKV_G_W), lambda bi, i: (bi, 0, i, 0)),
        ],
        out_shape=[
            jax.ShapeDtypeStruct((b, s, NAT_W), _BF16),
            jax.ShapeDtypeStruct((b, 4, s // 4, QKV_G_W), _BF16),
            jax.ShapeDtypeStruct((b, 16, s // 16, QKV_G_W), _BF16),
        ],
        scratch_shapes=[pltpu.VMEM((D_MODEL // LANES, tm, LANES), _F32),
                        pltpu.VMEM((tm, D_MODEL), _BF16),
                        pltpu.VMEM((tm, D_MODEL), _BF16), pltpu.VMEM((tm, D_MODEL), _BF16)],
        compiler_params=pltpu.CompilerParams(
            dimension_semantics=("arbitrary", "arbitrary"), vmem_limit_bytes=VMEM_LIMIT),
        name="in_proj",
    )(x, gain, w_all)


def _attn_kernel(q_ref, kp_ref, km_ref, kn_ref, vp_ref, vm_ref, vn_ref, bias_ref,
                 o_ref, st_ref, *, n_steps):
    n_res, q_step = q_ref.shape[0], q_ref.shape[1]
    j = pl.program_id(2)
    lane = lax.broadcasted_iota(jnp.int32, (Q_SUB, LANES), 1)
    low_head = lane < HEAD_DIM
    n_sub = q_step // Q_SUB
    ones = jnp.ones((K_SUB, LANES), _BF16)

    def window(prev_ref, main_ref, next_ref, res, i, cs):
        if i == 0:
            return jnp.concatenate(
                [prev_ref[res, :, cs], main_ref[res, 0:K_SUB - N_SIDE, cs]], axis=0)
        if i == n_sub - 1:
            return jnp.concatenate(
                [main_ref[res, q_step - (K_SUB - N_SIDE):q_step, cs], next_ref[res, :, cs]], axis=0)
        return main_ref[res, i * Q_SUB - N_SIDE:i * Q_SUB - N_SIDE + K_SUB, cs]

    for res in range(n_res):
        for i in range(n_sub):
            r0 = i * Q_SUB
            if i == 0:
                var = jnp.where(j == 0, 1, 0)
            elif i == n_sub - 1:
                var = jnp.where(j == n_steps - 1, 2, 0)
            else:
                var = 0
            stats = jnp.zeros((Q_SUB, LANES), _F32)
            for hp in range(HEADS_PER_GROUP // 2):
                cs = slice(hp * LANES, (hp + 1) * LANES)
                q2 = q_ref[res, r0:r0 + Q_SUB, cs]
                k2 = window(kp_ref, km_ref, kn_ref, res, i, cs)
                v2 = jnp.concatenate([window(vp_ref, vm_ref, vn_ref, res, i, cs), ones], axis=1)
                zero = jnp.zeros_like(q2)
                qs = jnp.concatenate([jnp.where(low_head, q2, zero),
                                      jnp.where(low_head, zero, q2)], axis=0)
                s = lax.dot_general(qs, k2, (((1,), (1,)), ((), ())),
                                    preferred_element_type=_F32)
                s = s + bias_ref[var, hp]
                m = jnp.max(s, axis=-1, keepdims=True)
                p = jnp.exp2(s - m)
                ol = jnp.dot(p.astype(_BF16), v2, preferred_element_type=_F32)
                o, l = ol[:, :LANES], ol[:, LANES:]
                o_ref[res, r0:r0 + Q_SUB, cs] = (
                    jnp.where(low_head, o[:Q_SUB], o[Q_SUB:]).astype(_BF16))
                for a in range(2):
                    h = 2 * hp + a
                    rows = slice(a * Q_SUB, (a + 1) * Q_SUB)
                    stats = jnp.where(lane == h, m[rows], stats)
                    stats = jnp.where(lane == HEADS_PER_GROUP + h, l[rows], stats)
            st_ref[res, r0:r0 + Q_SUB, :] = stats


def _attn_group(zg, bias, g, col0, n_res, q_step):
    b, dil, sp, _ = zg.shape
    assert dil % n_res == 0 and sp % q_step == 0 and q_step >= 2 * Q_SUB
    n_steps = sp // q_step
    hb = q_step // N_SIDE
    last_hb = sp // N_SIDE - 1

    def main(col, width=BRANCH_W):
        return pl.BlockSpec((None, n_res, q_step, width), lambda bi, r, j: (bi, r, j, col))

    def prev(col):
        return pl.BlockSpec((None, n_res, N_SIDE, BRANCH_W),
                            lambda bi, r, j: (bi, r, jnp.maximum(j * hb - 1, 0), col))

    def nxt(col):
        return pl.BlockSpec((None, n_res, N_SIDE, BRANCH_W),
                            lambda bi, r, j: (bi, r, jnp.minimum((j + 1) * hb, last_hb), col))

    qc, kc, vc = col0, col0 + 1, col0 + 2
    return pl.pallas_call(
        functools.partial(_attn_kernel, n_steps=n_steps),
        grid=(b, dil // n_res, n_steps),
        in_specs=[main(qc), prev(kc), main(kc), nxt(kc), prev(vc), main(vc), nxt(vc),
                  pl.BlockSpec((None,) + bias.shape[1:], lambda bi, r, j: (g, 0, 0, 0, 0))],
        out_specs=[main(0), main(0, LANES)],
        out_shape=[
            jax.ShapeDtypeStruct((b, dil, sp, BRANCH_W), _BF16),
            jax.ShapeDtypeStruct((b, dil, sp, LANES), _F32),
        ],
        compiler_params=pltpu.CompilerParams(
            dimension_semantics=("arbitrary", "arbitrary", "arbitrary"),
            vmem_limit_bytes=VMEM_LIMIT),
        name=f"attn_d{dil}",
    )(zg, zg, zg, zg, zg, zg, zg, bias)


def _t5_bucket(rel):
    nb = NUM_BUCKETS // 2
    ret = (rel > 0).astype(jnp.int32) * nb
    n = jnp.abs(rel)
    max_exact = nb // 2
    nf = jnp.maximum(n, 1).astype(_F32)
    large = max_exact + (jnp.log(nf / max_exact) / math.log(MAX_DISTANCE / max_exact)
                         * (nb - max_exact)).astype(jnp.int32)
    large = jnp.minimum(large, nb - 1)
    return ret + jnp.where(n < max_exact, n, large)


def _band_bias(rel_bias):
    n_off = 2 * N_SIDE + 1
    n_heads = N_ATT_GROUPS * HEADS_PER_GROUP
    dils = jnp.asarray([dil for _, dil in DILATED_GROUPS], jnp.int32)[:, None]
    offs = jnp.arange(-N_SIDE, N_SIDE + 1, dtype=jnp.int32)[None, :] * dils
    by_group = rel_bias.reshape(NUM_BUCKETS, N_ATT_GROUPS, HEADS_PER_GROUP)
    per_off = by_group[_t5_bucket(offs), jnp.arange(N_ATT_GROUPS)[:, None]]
    per_off = jnp.swapaxes(per_off, 1, 2).reshape(n_heads, n_off).astype(_F32) * LOG2_E
    ext = jnp.concatenate(
        [per_off, jnp.full((n_heads, K_SUB + 1 - n_off), NEG_INF, _F32)], axis=1)
    band = jnp.tile(ext, (1, Q_SUB))[:, :Q_SUB * K_SUB].reshape(
        N_ATT_GROUPS, 1, HEADS_PER_GROUP, Q_SUB, K_SUB)
    kj = jnp.arange(K_SUB, dtype=jnp.int32)
    first = jnp.where(kj < N_SIDE, NEG_INF, band)
    last = jnp.where(kj >= K_SUB - N_SIDE, NEG_INF, band)
    return jnp.concatenate([band, first, last], axis=1).reshape(
        N_ATT_GROUPS, 3, HEADS_PER_GROUP // 2, 2 * Q_SUB, K_SUB)


def _silu_of_half(h):
    return h + h * jnp.tanh(h)


def _epilogue_kernel(x_ref, za_ref, zprev_ref, znext_ref, zga_ref, zgb_ref, zbg_ref,
                     o1_ref, o4_ref, o16_ref, s1_ref, s4_ref, s16_ref,
                     icount_in, pscale_ref, bgate_ref, fgain_ref,
                     pmat_in, wpool_in, wpa_in, wpb_in, wout_in, expand_in,
                     out_ref,
                     ext_ref, o4n_ref, o16n_ref, s4n_ref, s16n_ref, icount_ref,
                     pmat_ref, wpool_ref, wpa_ref, wpb_ref, wout_ref, expand_ref, *, seq_len):
    tm = x_ref.shape[0]
    p0 = pl.program_id(1) * tm

    @pl.when(jnp.logical_and(pl.program_id(0) == 0, pl.program_id(1) == 0))
    def _stage_weights():
        for src, dst in ((pmat_in, pmat_ref), (wpa_in, wpa_ref), (wpb_in, wpb_ref),
                         (expand_in, expand_ref)):
            dst[...] = src[...].astype(_BF16)
        wout_ref[...] = (0.5 * wout_in[...]).astype(_BF16)
        wpool_ref[...] = jnp.zeros_like(wpool_ref)
        for gi in range(len(POOL_WINDOWS)):
            d0 = (gi % 2) * POOL_GROUP_W
            wpool_ref[gi // 2, d0:d0 + POOL_GROUP_W, d0:d0 + POOL_GROUP_W] = (
                wpool_in[gi].astype(_BF16))
        icount_ref[...] = jnp.broadcast_to(icount_in[0, 0:1, :], icount_ref.shape)

    icount_ref[0:HALO_ROWS] = jnp.where(p0 == 0, icount_in[1], icount_in[0])
    icount_ref[tm - HALO_ROWS:tm] = jnp.where(p0 + tm == seq_len, icount_in[2], icount_in[0])

    has_prev = (p0 > 0).astype(_BF16)
    has_next = (p0 + tm < seq_len).astype(_BF16)
    ext_ref[0:HALO_ROWS] = zprev_ref[...] * has_prev
    ext_ref[HALO_ROWS:HALO_ROWS + tm] = za_ref[:, 0:BRANCH_W]
    ext_ref[HALO_ROWS + tm:] = znext_ref[...] * has_next

    for dil, o_ref, on_ref, s_ref, sn_ref in ((4, o4_ref, o4n_ref, s4_ref, s4n_ref),
                                              (16, o16_ref, o16n_ref, s16_ref, s16n_ref)):
        rows = tm // dil
        for res in range(dil):
            o_res = o_ref[res].astype(_F32)
            for c in range(BRANCH_W // LANES):
                on_ref[c, pl.ds(res, rows, stride=dil), :] = o_res[:, c * LANES:(c + 1) * LANES]
            sn_ref[pl.ds(res, rows, stride=dil), :] = s_ref[res]

    n_chunks = tm // POOL_ROWS
    edge = [jnp.where(p0 == 0, 1, 0)] + [0] * (n_chunks - 2) + [jnp.where(p0 + tm == seq_len, 2, 0)]
    pooled = []
    for gi in range(len(POOL_WINDOWS)):
        cs = slice(gi * POOL_GROUP_W, (gi + 1) * POOL_GROUP_W)
        win = jnp.concatenate(
            [jnp.dot(pmat_ref[edge[ci], gi], ext_ref[ci * POOL_ROWS:ci * POOL_ROWS + POOL_K, cs],
                     preferred_element_type=_F32) for ci in range(n_chunks)], axis=0)
        pooled.append((win * icount_ref[:, cs]).astype(_BF16))
    mixed = [jnp.dot(jnp.concatenate(pooled[2 * pr:2 * pr + 2], axis=-1), wpool_ref[pr],
                     preferred_element_type=_F32) for pr in range(len(POOL_WINDOWS) // 2)]
    mixed = jnp.concatenate(mixed, axis=-1) * pscale_ref[...]
    ya_in = mixed * _silu_of_half(za_ref[:, BRANCH_W:].astype(_F32))
    y_a = jnp.dot(ya_in.astype(_BF16), wpa_ref[...], preferred_element_type=_F32)

    st = (s1_ref[...], s4n_ref[...], s16n_ref[...])
    mx = jnp.maximum(jnp.maximum(st[0], st[1]), st[2])
    es = [jnp.exp2(t - mx) for t in st]
    ls = [pltpu.roll(t, LANES - HEADS_PER_GROUP, 1) for t in st]
    head_lane = lax.broadcasted_iota(jnp.int32, (tm, LANES), 1) < HEADS_PER_GROUP
    den = jnp.where(head_lane, es[0] * ls[0] + es[1] * ls[1] + es[2] * ls[2], 1.0)
    inv = 1.0 / den
    packed = jnp.zeros((tm, LANES), _F32)
    for g, e in enumerate(es):
        wgt = jnp.where(head_lane, e * inv, 0.0)
        hi_part = wgt.astype(_BF16).astype(_F32)
        for part, lane0 in ((hi_part, HEADS_PER_GROUP * g),
                            (wgt - hi_part, HEADS_PER_GROUP * (N_ATT_GROUPS + g))):
            packed = packed + (pltpu.roll(part, lane0, 1) if lane0 else part)
    wide = jnp.dot(packed.astype(_BF16), expand_ref[...], preferred_element_type=_F32)
    ogs = (o1_ref[...].astype(_F32),
           jnp.concatenate([o4n_ref[c] for c in range(BRANCH_W // LANES)], axis=-1),
           jnp.concatenate([o16n_ref[c] for c in range(BRANCH_W // LANES)], axis=-1))
    att = sum(wide[:, g * BRANCH_W:(g + 1) * BRANCH_W] * og for g, og in enumerate(ogs))
    yb_in = att * _silu_of_half(zbg_ref[...].astype(_F32))
    y_b = jnp.dot(yb_in.astype(_BF16), wpb_ref[...], preferred_element_type=_F32)

    t_a = jnp.tanh(zga_ref[...].astype(_F32) + bgate_ref[0:1, :])
    t_b = jnp.tanh(zgb_ref[...].astype(_F32) + bgate_ref[1:2, :])
    merged2 = (y_a + y_b) + (t_a * y_a + t_b * y_b)
    xo = x_ref[...] + jnp.dot(merged2.astype(_BF16), wout_ref[...], preferred_element_type=_F32)
    r = lax.rsqrt(jnp.mean(xo * xo, axis=-1, keepdims=True) + RMS_EPS)
    out_ref[...] = (xo * r) * fgain_ref[...]


def _window_counts(seq_len, pos):
    pos = np.asarray(pos)[:, None]
    half = np.asarray([w // 2 for w in POOL_WINDOWS])[None, :]
    return np.minimum(pos + half - 1, seq_len - 1) - np.maximum(pos - half, 0) + 1


def _pool_matrices(seq_len):
    r = np.arange(POOL_ROWS)[:, None]
    c = np.arange(POOL_K)[None, :] - HALO_ROWS
    mid = seq_len // 2
    starts = (mid, 0, seq_len - POOL_ROWS)
    out = np.zeros((3, len(POOL_WINDOWS), POOL_ROWS, POOL_K), np.float32)
    for v, start in enumerate(starts):
        counts = _window_counts(seq_len, range(start, start + POOL_ROWS))
        for gi, w in enumerate(POOL_WINDOWS):
            window = (c - r >= -(w // 2)) & (c - r <= w // 2 - 1)
            out[v, gi] = window - counts[:, gi:gi + 1] * (c == r)
    return out


def _inv_window_counts(seq_len):
    mid = seq_len // 2
    rows = lambda start: np.repeat(
        1.0 / _window_counts(seq_len, range(start, start + HALO_ROWS)), POOL_GROUP_W, axis=1)
    return np.stack([rows(mid), rows(0), rows(seq_len - HALO_ROWS)]).astype(np.float32)


def _expand_matrix():
    row = np.arange(LANES)[:, None]
    col = np.arange(N_ATT_GROUPS * BRANCH_W)[None, :]
    n_packed = N_ATT_GROUPS * HEADS_PER_GROUP
    hit = ((row < 2 * n_packed)
           & ((row % n_packed) // HEADS_PER_GROUP == col // BRANCH_W)
           & (row % HEADS_PER_GROUP == (col % BRANCH_W) // HEAD_DIM))
    return hit.astype(np.float32)


def _epilogue(x, z_nat, outs, stats, w_pool, pool_scale, w_pa, w_pb, w_out, b_gate_half,
              final_gain):
    b, s, _ = x.shape
    tm = EPI_ROWS
    halo_per_tile = tm // HALO_ROWS
    last_halo = s // HALO_ROWS - 1
    pair_w = 2 * POOL_GROUP_W
    staged = ((jnp.asarray(_pool_matrices(s), _BF16), None),
              (w_pool, (len(POOL_WINDOWS) // 2, pair_w, pair_w)),
              (w_pa, None), (w_pb, None), (w_out, None),
              (jnp.asarray(_expand_matrix(), _BF16), None))

    def full(a):
        return pl.BlockSpec(a.shape, lambda bi, i: (0,) * a.ndim)

    def once(a):
        return pl.BlockSpec(a.shape, lambda bi, i: (0,) * a.ndim, pipeline_mode=pl.Buffered(1))

    def row(w, col):
        return pl.BlockSpec((None, tm, w), lambda bi, i: (bi, i, col))

    def by_residue(a):
        dil, w = a.shape[1], a.shape[3]
        return pl.BlockSpec((None, dil, tm // dil, w), lambda bi, i: (bi, 0, i, 0))

    in_specs = [
        row(D_MODEL, 0),
        row(2 * BRANCH_W, 0),
        pl.BlockSpec((None, HALO_ROWS, BRANCH_W),
                     lambda bi, i: (bi, jnp.maximum(i * halo_per_tile - 1, 0), 0)),
        pl.BlockSpec((None, HALO_ROWS, BRANCH_W),
                     lambda bi, i: (bi, jnp.minimum((i + 1) * halo_per_tile, last_halo), 0)),
        row(D_MODEL, 1), row(D_MODEL, 2),
        row(BRANCH_W, NAT_ATT_GATE),
        pl.BlockSpec((None, None, tm, BRANCH_W), lambda bi, i: (bi, 0, i, 0)),
        by_residue(outs[1]), by_residue(outs[2]),
        pl.BlockSpec((None, None, tm, LANES), lambda bi, i: (bi, 0, i, 0)),
        by_residue(stats[1]), by_residue(stats[2]),
        pl.BlockSpec((3, HALO_ROWS, BRANCH_W), lambda bi, i: (0, 0, 0)),
        full(pool_scale), full(b_gate_half), full(final_gain),
    ] + [once(a) for a, _ in staged]
    return pl.pallas_call(
        functools.partial(_epilogue_kernel, seq_len=s),
        grid=(b, s // tm),
        in_specs=in_specs,
        out_specs=pl.BlockSpec((None, tm, D_MODEL), lambda bi, i: (bi, i, 0)),
        out_shape=jax.ShapeDtypeStruct((b, s, D_MODEL), _F32),
        scratch_shapes=[pltpu.VMEM((tm + 2 * HALO_ROWS, BRANCH_W), _BF16),
                        pltpu.VMEM((BRANCH_W // LANES, tm, LANES), _F32),
                        pltpu.VMEM((BRANCH_W // LANES, tm, LANES), _F32),
                        pltpu.VMEM((tm, LANES), _F32), pltpu.VMEM((tm, LANES), _F32),
                        pltpu.VMEM((tm, BRANCH_W), _F32)]
        + [pltpu.VMEM(shape or a.shape, _BF16) for a, shape in staged],
        compiler_params=pltpu.CompilerParams(
            dimension_semantics=("arbitrary", "arbitrary"), vmem_limit_bytes=VMEM_LIMIT),
        name="epilogue",
    )(x, z_nat, z_nat, z_nat, z_nat, z_nat, z_nat, *outs, *stats,
      jnp.asarray(_inv_window_counts(s)), pool_scale, b_gate_half, final_gain,
      *[a for a, _ in staged])


def _split_w_in(w):
    def qkv(g):
        q, k, v = (w[:, c + g * BRANCH_W:c + (g + 1) * BRANCH_W] for c in (COL_Q, COL_K, COL_V))
        return [q * SCORE_SCALE, k, v]
    nat = [w[:, :BRANCH_W], 0.5 * w[:, BRANCH_W:COL_Q], 0.5 * w[:, COL_MERGE:],
           0.5 * w[:, COL_ATT_GATE:COL_MERGE]] + qkv(0)
    return jnp.concatenate(nat + qkv(1) + qkv(2), axis=1).astype(_BF16)


def kernel(x, norm_gain, w_in, b_gate, rel_bias, w_pool, pool_scale, w_proj_a, w_proj_b,
           w_out, final_gain):
    b, s, d = x.shape
    assert d == D_MODEL and norm_gain.shape[0] == 1, "single-layer block of width D_MODEL"
    assert s % EPI_ROWS == 0 and s % IN_PROJ_ROWS == 0
    assert [dil for _, dil in DILATED_GROUPS] == [1, 4, 16]
    z_nat, z_d4, z_d16 = _in_proj(x, norm_gain[0][None, :], _split_w_in(w_in[0]))
    bias = _band_bias(rel_bias)
    outs, stats = [], []
    for g, (zg, col0) in enumerate(((z_nat[:, None], NAT_QKV), (z_d4, 0), (z_d16, 0))):
        o, st = _attn_group(zg, bias, g, col0, *ATTN_STEP[g])
        outs.append(o)
        stats.append(st)
    return _epilogue(x, z_nat, outs, stats, w_pool[0], pool_scale[0][None, :],
                     w_proj_a[0], w_proj_b[0], w_out[0], 0.5 * b_gate[0], final_gain[None, :])
```

```python
import functools
import math

import jax
import jax.numpy as jnp
import numpy as np
from jax import lax
from jax.experimental import pallas as pl
from jax.experimental.pallas import tpu as pltpu

D_MODEL = 1024
BRANCH_W = D_MODEL // 2
POOL_WINDOWS = (2, 4, 8, 16)
POOL_GROUP_W = BRANCH_W // len(POOL_WINDOWS)
DILATED_GROUPS = ((128, 1), (512, 4), (2048, 16))
N_ATT_GROUPS = len(DILATED_GROUPS)
HEAD_DIM = 64
HEADS_PER_GROUP = BRANCH_W // HEAD_DIM
QKV_W = N_ATT_GROUPS * BRANCH_W
NUM_BUCKETS = 32
MAX_DISTANCE = 1024
IN_W = 2 * BRANCH_W + 3 * QKV_W + BRANCH_W + 2 * D_MODEL
RMS_EPS = 1e-6
NEG_INF = -1e30
LOG2_E = math.log2(math.e)
SCORE_SCALE = LOG2_E / math.sqrt(HEAD_DIM)

COL_Q = 2 * BRANCH_W
COL_K = COL_Q + QKV_W
COL_V = COL_K + QKV_W
COL_ATT_GATE = COL_V + QKV_W
COL_MERGE = COL_ATT_GATE + BRANCH_W

NAT_ATT_GATE = 6
NAT_QKV = 7
NAT_W = 10 * BRANCH_W
QKV_G_W = 3 * BRANCH_W

N_SIDE = 64
LANES = 128
Q_SUB = 128
K_SUB = Q_SUB + 2 * N_SIDE
ATTN_STEP = ((1, 2048), (1, 2048), (4, 512))
HALO_ROWS = 16
POOL_ROWS = 128
POOL_K = POOL_ROWS + 2 * HALO_ROWS

IN_PROJ_ROWS = 512
IN_PROJ_COLS = 512
EPI_ROWS = 512
V7X_VMEM_BYTES = 64 * 1024 * 1024
VMEM_LIMIT = V7X_VMEM_BYTES * 7 // 8

_BF16 = jnp.bfloat16
_F32 = jnp.float32


def _in_proj_kernel(x_ref, g_ref, w_ref, zn_ref, z4_ref, z16_ref,
                    hf_ref, hn_ref, h4_ref, h16_ref):
    tm = x_ref.shape[0]
    x = x_ref[...]
    r = lax.rsqrt(jnp.mean(x * x, axis=-1, keepdims=True) + RMS_EPS)
    hf = (x * r) * g_ref[...]
    n_slab = D_MODEL // LANES
    for c in range(n_slab):
        hf_ref[c] = hf[:, c * LANES:(c + 1) * LANES]
    hn_ref[...] = hf.astype(_BF16)
    for c in range(NAT_W // IN_PROJ_COLS):
        cs = slice(c * IN_PROJ_COLS, (c + 1) * IN_PROJ_COLS)
        zn_ref[:, cs] = jnp.dot(hn_ref[...], w_ref[:, cs],
                                preferred_element_type=_F32).astype(_BF16)
    for dil, h_ref, col0, z_ref in ((4, h4_ref, NAT_W, z4_ref),
                                    (16, h16_ref, NAT_W + QKV_G_W, z16_ref)):
        rows = tm // dil
        for res in range(dil):
            for c in range(n_slab):
                h_ref[res * rows:(res + 1) * rows, c * LANES:(c + 1) * LANES] = (
                    hf_ref[c, pl.ds(res, rows, stride=dil), :].astype(_BF16))
        for c in range(QKV_G_W // IN_PROJ_COLS):
            cs = slice(c * IN_PROJ_COLS, (c + 1) * IN_PROJ_COLS)
            zc = jnp.dot(h_ref[...], w_ref[:, col0 + cs.start:col0 + cs.stop],
                         preferred_element_type=_F32).astype(_BF16)
            for res in range(dil):
                z_ref[res, :, cs] = zc[res * rows:(res + 1) * rows]


def _in_proj(x, gain, w_all):
    b, s, _ = x.shape
    tm = IN_PROJ_ROWS
    return pl.pallas_call(
        _in_proj_kernel,
        grid=(b, s // tm),
        in_specs=[
            pl.BlockSpec((None, tm, D_MODEL), lambda bi, i: (bi, i, 0)),
            pl.BlockSpec((1, D_MODEL), lambda bi, i: (0, 0)),
            pl.BlockSpec(w_all.shape, lambda bi, i: (0, 0), pipeline_mode=pl.Buffered(1)),
        ],
        out_specs=[
            pl.BlockSpec((None, tm, NAT_W), lambda bi, i: (bi, i, 0)),
            pl.BlockSpec((None, 4, tm // 4, QKV_G_W), lambda bi, i: (bi, 0, i, 0)),
            pl.BlockSpec((None, 16, tm // 16, QKV_G_W), lambda bi, i: (bi, 0, i, 0)),
        ],
        out_shape=[
            jax.ShapeDtypeStruct((b, s, NAT_W), _BF16),
            jax.ShapeDtypeStruct((b, 4, s // 4, QKV_G_W), _BF16),
            jax.ShapeDtypeStruct((b, 16, s // 16, QKV_G_W), _BF16),
        ],
        scratch_shapes=[pltpu.VMEM((D_MODEL // LANES, tm, LANES), _F32),
                        pltpu.VMEM((tm, D_MODEL), _BF16),
                        pltpu.VMEM((tm, D_MODEL), _BF16), pltpu.VMEM((tm, D_MODEL), _BF16)],
        compiler_params=pltpu.CompilerParams(
            dimension_semantics=("arbitrary", "arbitrary"), vmem_limit_bytes=VMEM_LIMIT),
        name="in_proj",
    )(x, gain, w_all)


def _attn_kernel(q_ref, kp_ref, km_ref, kn_ref, vp_ref, vm_ref, vn_ref, per_off_ref,
                 o_ref, st_ref, bias_ref, *, n_steps):
    n_res, q_step = q_ref.shape[0], q_ref.shape[1]
    j = pl.program_id(2)

    @pl.when(jnp.logical_and(jnp.logical_and(pl.program_id(0) == 0, pl.program_id(1) == 0),
                             j == 0))
    def _build_bias():
        key = lax.broadcasted_iota(jnp.int32, (Q_SUB, K_SUB), 1)
        for h in range(HEADS_PER_GROUP):
            rows = slice((h % 2) * Q_SUB, (h % 2 + 1) * Q_SUB)
            band = pltpu.roll(jnp.broadcast_to(per_off_ref[h:h + 1, :], (Q_SUB, K_SUB)),
                              0, 1, stride=1, stride_axis=0)
            bias_ref[0, h // 2, rows, :] = band
            bias_ref[1, h // 2, rows, :] = jnp.where(key < N_SIDE, NEG_INF, band)
            bias_ref[2, h // 2, rows, :] = jnp.where(key >= K_SUB - N_SIDE, NEG_INF, band)

    lane = lax.broadcasted_iota(jnp.int32, (Q_SUB, LANES), 1)
    low_head = lane < HEAD_DIM
    n_sub = q_step // Q_SUB
    ones = jnp.ones((K_SUB, LANES), _BF16)

    def window(prev_ref, main_ref, next_ref, res, i, cs):
        if i == 0:
            return jnp.concatenate(
                [prev_ref[res, :, cs], main_ref[res, 0:K_SUB - N_SIDE, cs]], axis=0)
        if i == n_sub - 1:
            return jnp.concatenate(
                [main_ref[res, q_step - (K_SUB - N_SIDE):q_step, cs], next_ref[res, :, cs]], axis=0)
        return main_ref[res, i * Q_SUB - N_SIDE:i * Q_SUB - N_SIDE + K_SUB, cs]

    for res in range(n_res):
        for i in range(n_sub):
            r0 = i * Q_SUB
            if i == 0:
                var = jnp.where(j == 0, 1, 0)
            elif i == n_sub - 1:
                var = jnp.where(j == n_steps - 1, 2, 0)
            else:
                var = 0
            stats = jnp.zeros((Q_SUB, LANES), _F32)
            for hp in range(HEADS_PER_GROUP // 2):
                cs = slice(hp * LANES, (hp + 1) * LANES)
                q2 = q_ref[res, r0:r0 + Q_SUB, cs]
                k2 = window(kp_ref, km_ref, kn_ref, res, i, cs)
                v2 = jnp.concatenate([window(vp_ref, vm_ref, vn_ref, res, i, cs), ones], axis=1)
                zero = jnp.zeros_like(q2)
                qs = jnp.concatenate([jnp.where(low_head, q2, zero),
                                      jnp.where(low_head, zero, q2)], axis=0)
                s = lax.dot_general(qs, k2, (((1,), (1,)), ((), ())),
                                    preferred_element_type=_F32)
                s = s + bias_ref[var, hp]
                m = jnp.max(s, axis=-1, keepdims=True)
                p = jnp.exp2(s - m)
                ol = jnp.dot(p.astype(_BF16), v2, preferred_element_type=_F32)
                o, l = ol[:, :LANES], ol[:, LANES:]
                o_ref[res, r0:r0 + Q_SUB, cs] = (
                    jnp.where(low_head, o[:Q_SUB], o[Q_SUB:]).astype(_BF16))
                for a in range(2):
                    h = 2 * hp + a
                    rows = slice(a * Q_SUB, (a + 1) * Q_SUB)
                    stats = jnp.where(lane == h, m[rows], stats)
                    stats = jnp.where(lane == HEADS_PER_GROUP + h, l[rows], stats)
            st_ref[res, r0:r0 + Q_SUB, :] = stats


def _attn_group(zg, per_off, g, col0, n_res, q_step):
    b, dil, sp, _ = zg.shape
    assert dil % n_res == 0 and sp % q_step == 0 and q_step >= 2 * Q_SUB
    n_steps = sp // q_step
    hb = q_step // N_SIDE
    last_hb = sp // N_SIDE - 1

    def main(col, width=BRANCH_W):
        return pl.BlockSpec((None, n_res, q_step, width), lambda bi, r, j: (bi, r, j, col))

    def prev(col):
        return pl.BlockSpec((None, n_res, N_SIDE, BRANCH_W),
                            lambda bi, r, j: (bi, r, jnp.maximum(j * hb - 1, 0), col))

    def nxt(col):
        return pl.BlockSpec((None, n_res, N_SIDE, BRANCH_W),
                            lambda bi, r, j: (bi, r, jnp.minimum((j + 1) * hb, last_hb), col))

    qc, kc, vc = col0, col0 + 1, col0 + 2
    return pl.pallas_call(
        functools.partial(_attn_kernel, n_steps=n_steps),
        grid=(b, dil // n_res, n_steps),
        in_specs=[main(qc), prev(kc), main(kc), nxt(kc), prev(vc), main(vc), nxt(vc),
                  pl.BlockSpec((None,) + per_off.shape[1:], lambda bi, r, j: (g, 0, 0))],
        out_specs=[main(0), main(0, LANES)],
        out_shape=[
            jax.ShapeDtypeStruct((b, dil, sp, BRANCH_W), _BF16),
            jax.ShapeDtypeStruct((b, dil, sp, LANES), _F32),
        ],
        scratch_shapes=[pltpu.VMEM((3, HEADS_PER_GROUP // 2, 2 * Q_SUB, K_SUB), _F32)],
        compiler_params=pltpu.CompilerParams(
            dimension_semantics=("arbitrary", "arbitrary", "arbitrary"),
            vmem_limit_bytes=VMEM_LIMIT),
        name=f"attn_d{dil}",
    )(zg, zg, zg, zg, zg, zg, zg, per_off)


def _t5_bucket(rel):
    nb = NUM_BUCKETS // 2
    ret = (rel > 0).astype(jnp.int32) * nb
    n = jnp.abs(rel)
    max_exact = nb // 2
    nf = jnp.maximum(n, 1).astype(_F32)
    large = max_exact + (jnp.log(nf / max_exact) / math.log(MAX_DISTANCE / max_exact)
                         * (nb - max_exact)).astype(jnp.int32)
    large = jnp.minimum(large, nb - 1)
    return ret + jnp.where(n < max_exact, n, large)


def _bias_per_offset(rel_bias):
    n_off = 2 * N_SIDE + 1
    dils = jnp.asarray([dil for _, dil in DILATED_GROUPS], jnp.int32)[:, None]
    offs = jnp.arange(-N_SIDE, N_SIDE + 1, dtype=jnp.int32)[None, :] * dils
    by_group = rel_bias.reshape(NUM_BUCKETS, N_ATT_GROUPS, HEADS_PER_GROUP)
    per_off = by_group[_t5_bucket(offs), jnp.arange(N_ATT_GROUPS)[:, None]]
    per_off = jnp.swapaxes(per_off, 1, 2).astype(_F32) * LOG2_E
    pad = jnp.full((N_ATT_GROUPS, HEADS_PER_GROUP, K_SUB - n_off), NEG_INF, _F32)
    return jnp.concatenate([per_off, pad], axis=2)


def _silu_of_half(h):
    return h + h * jnp.tanh(h)


def _epilogue_kernel(x_ref, za_ref, zprev_ref, znext_ref, zga_ref, zgb_ref, zbg_ref,
                     o1_ref, o4_ref, o16_ref, s1_ref, s4_ref, s16_ref,
                     icount_in, pscale_ref, bgate_ref, fgain_ref,
                     pmat_in, wpool_in, wpa_in, wpb_in, wout_in, expand_in,
                     out_ref,
                     ext_ref, o4n_ref, o16n_ref, s4n_ref, s16n_ref, icount_ref,
                     pmat_ref, wpool_ref, wpa_ref, wpb_ref, wout_ref, expand_ref, *, seq_len):
    tm = x_ref.shape[0]
    p0 = pl.program_id(1) * tm

    @pl.when(jnp.logical_and(pl.program_id(0) == 0, pl.program_id(1) == 0))
    def _stage_weights():
        for src, dst in ((pmat_in, pmat_ref), (wpa_in, wpa_ref), (wpb_in, wpb_ref),
                         (expand_in, expand_ref)):
            dst[...] = src[...].astype(_BF16)
        wout_ref[...] = (0.5 * wout_in[...]).astype(_BF16)
        wpool_ref[...] = jnp.zeros_like(wpool_ref)
        for gi in range(len(POOL_WINDOWS)):
            d0 = (gi % 2) * POOL_GROUP_W
            wpool_ref[gi // 2, d0:d0 + POOL_GROUP_W, d0:d0 + POOL_GROUP_W] = (
                wpool_in[gi].astype(_BF16))
        icount_ref[...] = jnp.broadcast_to(icount_in[0, 0:1, :], icount_ref.shape)

    icount_ref[0:HALO_ROWS] = jnp.where(p0 == 0, icount_in[1], icount_in[0])
    icount_ref[tm - HALO_ROWS:tm] = jnp.where(p0 + tm == seq_len, icount_in[2], icount_in[0])

    has_prev = (p0 > 0).astype(_BF16)
    has_next = (p0 + tm < seq_len).astype(_BF16)
    ext_ref[0:HALO_ROWS] = zprev_ref[...] * has_prev
    ext_ref[HALO_ROWS:HALO_ROWS + tm] = za_ref[:, 0:BRANCH_W]
    ext_ref[HALO_ROWS + tm:] = znext_ref[...] * has_next

    for dil, o_ref, on_ref, s_ref, sn_ref in ((4, o4_ref, o4n_ref, s4_ref, s4n_ref),
                                              (16, o16_ref, o16n_ref, s16_ref, s16n_ref)):
        rows = tm // dil
        for res in range(dil):
            o_res = o_ref[res].astype(_F32)
            for c in range(BRANCH_W // LANES):
                on_ref[c, pl.ds(res, rows, stride=dil), :] = o_res[:, c * LANES:(c + 1) * LANES]
            sn_ref[pl.ds(res, rows, stride=dil), :] = s_ref[res]

    n_chunks = tm // POOL_ROWS
    edge = [jnp.where(p0 == 0, 1, 0)] + [0] * (n_chunks - 2) + [jnp.where(p0 + tm == seq_len, 2, 0)]
    pooled = []
    for gi in range(len(POOL_WINDOWS)):
        cs = slice(gi * POOL_GROUP_W, (gi + 1) * POOL_GROUP_W)
        win = jnp.concatenate(
            [jnp.dot(pmat_ref[edge[ci], gi], ext_ref[ci * POOL_ROWS:ci * POOL_ROWS + POOL_K, cs],
                     preferred_element_type=_F32) for ci in range(n_chunks)], axis=0)
        pooled.append((win * icount_ref[:, cs]).astype(_BF16))
    mixed = [jnp.dot(jnp.concatenate(pooled[2 * pr:2 * pr + 2], axis=-1), wpool_ref[pr],
                     preferred_element_type=_F32) for pr in range(len(POOL_WINDOWS) // 2)]
    mixed = jnp.concatenate(mixed, axis=-1) * pscale_ref[...]
    ya_in = mixed * _silu_of_half(za_ref[:, BRANCH_W:].astype(_F32))
    y_a = jnp.dot(ya_in.astype(_BF16), wpa_ref[...], preferred_element_type=_F32)

    st = (s1_ref[...], s4n_ref[...], s16n_ref[...])
    mx = jnp.maximum(jnp.maximum(st[0], st[1]), st[2])
    es = [jnp.exp2(t - mx) for t in st]
    ls = [pltpu.roll(t, LANES - HEADS_PER_GROUP, 1) for t in st]
    head_lane = lax.broadcasted_iota(jnp.int32, (tm, LANES), 1) < HEADS_PER_GROUP
    den = jnp.where(head_lane, es[0] * ls[0] + es[1] * ls[1] + es[2] * ls[2], 1.0)
    inv = 1.0 / den
    packed = jnp.zeros((tm, LANES), _F32)
    for g, e in enumerate(es):
        wgt = jnp.where(head_lane, e * inv, 0.0)
        hi_part = wgt.astype(_BF16).astype(_F32)
        for part, lane0 in ((hi_part, HEADS_PER_GROUP * g),
                            (wgt - hi_part, HEADS_PER_GROUP * (N_ATT_GROUPS + g))):
            packed = packed + (pltpu.roll(part, lane0, 1) if lane0 else part)
    wide = jnp.dot(packed.astype(_BF16), expand_ref[...], preferred_element_type=_F32)
    ogs = (o1_ref[...].astype(_F32),
           jnp.concatenate([o4n_ref[c] for c in range(BRANCH_W // LANES)], axis=-1),
           jnp.concatenate([o16n_ref[c] for c in range(BRANCH_W // LANES)], axis=-1))
    att = sum(wide[:, g * BRANCH_W:(g + 1) * BRANCH_W] * og for g, og in enumerate(ogs))
    yb_in = att * _silu_of_half(zbg_ref[...].astype(_F32))
    y_b = jnp.dot(yb_in.astype(_BF16), wpb_ref[...], preferred_element_type=_F32)

    t_a = jnp.tanh(zga_ref[...].astype(_F32) + bgate_ref[0:1, :])
    t_b = jnp.tanh(zgb_ref[...].astype(_F32) + bgate_ref[1:2, :])
    merged2 = (y_a + y_b) + (t_a * y_a + t_b * y_b)
    xo = x_ref[...] + jnp.dot(merged2.astype(_BF16), wout_ref[...], preferred_element_type=_F32)
    r = lax.rsqrt(jnp.mean(xo * xo, axis=-1, keepdims=True) + RMS_EPS)
    out_ref[...] = (xo * r) * fgain_ref[...]


def _window_counts(seq_len, pos):
    pos = np.asarray(pos)[:, None]
    half = np.asarray([w // 2 for w in POOL_WINDOWS])[None, :]
    return np.minimum(pos + half - 1, seq_len - 1) - np.maximum(pos - half, 0) + 1


def _pool_matrices(seq_len):
    r = np.arange(POOL_ROWS)[:, None]
    c = np.arange(POOL_K)[None, :] - HALO_ROWS
    mid = seq_len // 2
    starts = (mid, 0, seq_len - POOL_ROWS)
    out = np.zeros((3, len(POOL_WINDOWS), POOL_ROWS, POOL_K), np.float32)
    for v, start in enumerate(starts):
        counts = _window_counts(seq_len, range(start, start + POOL_ROWS))
        for gi, w in enumerate(POOL_WINDOWS):
            window = (c - r >= -(w // 2)) & (c - r <= w // 2 - 1)
            out[v, gi] = window - counts[:, gi:gi + 1] * (c == r)
    return out


def _inv_window_counts(seq_len):
    mid = seq_len // 2
    rows = lambda start: np.repeat(
        1.0 / _window_counts(seq_len, range(start, start + HALO_ROWS)), POOL_GROUP_W, axis=1)
    return np.stack([rows(mid), rows(0), rows(seq_len - HALO_ROWS)]).astype(np.float32)


def _expand_matrix():
    row = np.arange(LANES)[:, None]
    col = np.arange(N_ATT_GROUPS * BRANCH_W)[None, :]
    n_packed = N_ATT_GROUPS * HEADS_PER_GROUP
    hit = ((row < 2 * n_packed)
           & ((row % n_packed) // HEADS_PER_GROUP == col // BRANCH_W)
           & (row % HEADS_PER_GROUP == (col % BRANCH_W) // HEAD_DIM))
    return hit.astype(np.float32)


def _epilogue(x, z_nat, outs, stats, w_pool, pool_scale, w_pa, w_pb, w_out, b_gate_half,
              final_gain):
    b, s, _ = x.shape
    tm = EPI_ROWS
    halo_per_tile = tm // HALO_ROWS
    last_halo = s // HALO_ROWS - 1
    pair_w = 2 * POOL_GROUP_W
    staged = ((jnp.asarray(_pool_matrices(s), _BF16), None),
              (w_pool, (len(POOL_WINDOWS) // 2, pair_w, pair_w)),
              (w_pa, None), (w_pb, None), (w_out, None),
              (jnp.asarray(_expand_matrix(), _BF16), None))

    def full(a):
        return pl.BlockSpec(a.shape, lambda bi, i: (0,) * a.ndim)

    def once(a):
        return pl.BlockSpec(a.shape, lambda bi, i: (0,) * a.ndim, pipeline_mode=pl.Buffered(1))

    def row(w, col):
        return pl.BlockSpec((None, tm, w), lambda bi, i: (bi, i, col))

    def by_residue(a):
        dil, w = a.shape[1], a.shape[3]
        return pl.BlockSpec((None, dil, tm // dil, w), lambda bi, i: (bi, 0, i, 0))

    in_specs = [
        row(D_MODEL, 0),
        row(2 * BRANCH_W, 0),
        pl.BlockSpec((None, HALO_ROWS, BRANCH_W),
                     lambda bi, i: (bi, jnp.maximum(i * halo_per_tile - 1, 0), 0)),
        pl.BlockSpec((None, HALO_ROWS, BRANCH_W),
                     lambda bi, i: (bi, jnp.minimum((i + 1) * halo_per_tile, last_halo), 0)),
        row(D_MODEL, 1), row(D_MODEL, 2),
        row(BRANCH_W, NAT_ATT_GATE),
        pl.BlockSpec((None, None, tm, BRANCH_W), lambda bi, i: (bi, 0, i, 0)),
        by_residue(outs[1]), by_residue(outs[2]),
        pl.BlockSpec((None, None, tm, LANES), lambda bi, i: (bi, 0, i, 0)),
        by_residue(stats[1]), by_residue(stats[2]),
        pl.BlockSpec((3, HALO_ROWS, BRANCH_W), lambda bi, i: (0, 0, 0)),
        full(pool_scale), full(b_gate_half), full(final_gain),
    ] + [once(a) for a, _ in staged]
    return pl.pallas_call(
        functools.partial(_epilogue_kernel, seq_len=s),
        grid=(b, s // tm),
        in_specs=in_specs,
        out_specs=pl.BlockSpec((None, tm, D_MODEL), lambda bi, i: (bi, i, 0)),
        out_shape=jax.ShapeDtypeStruct((b, s, D_MODEL), _F32),
        scratch_shapes=[pltpu.VMEM((tm + 2 * HALO_ROWS, BRANCH_W), _BF16),
                        pltpu.VMEM((BRANCH_W // LANES, tm, LANES), _F32),
                        pltpu.VMEM((BRANCH_W // LANES, tm, LANES), _F32),
                        pltpu.VMEM((tm, LANES), _F32), pltpu.VMEM((tm, LANES), _F32),
                        pltpu.VMEM((tm, BRANCH_W), _F32)]
        + [pltpu.VMEM(shape or a.shape, _BF16) for a, shape in staged],
        compiler_params=pltpu.CompilerParams(
            dimension_semantics=("arbitrary", "arbitrary"), vmem_limit_bytes=VMEM_LIMIT),
        name="epilogue",
    )(x, z_nat, z_nat, z_nat, z_nat, z_nat, z_nat, *outs, *stats,
      jnp.asarray(_inv_window_counts(s)), pool_scale, b_gate_half, final_gain,
      *[a for a, _ in staged])


def _split_w_in(w):
    def qkv(g):
        q, k, v = (w[:, c + g * BRANCH_W:c + (g + 1) * BRANCH_W] for c in (COL_Q, COL_K, COL_V))
        return [q * SCORE_SCALE, k, v]
    nat = [w[:, :BRANCH_W], 0.5 * w[:, BRANCH_W:COL_Q], 0.5 * w[:, COL_MERGE:],
           0.5 * w[:, COL_ATT_GATE:COL_MERGE]] + qkv(0)
    return jnp.concatenate(nat + qkv(1) + qkv(2), axis=1).astype(_BF16)


def kernel(x, norm_gain, w_in, b_gate, rel_bias, w_pool, pool_scale, w_proj_a, w_proj_b,
           w_out, final_gain):
    b, s, d = x.shape
    assert d == D_MODEL and norm_gain.shape[0] == 1, "single-layer block of width D_MODEL"
    assert s % EPI_ROWS == 0 and s % IN_PROJ_ROWS == 0
    assert [dil for _, dil in DILATED_GROUPS] == [1, 4, 16]
    z_nat, z_d4, z_d16 = _in_proj(x, norm_gain[0][None, :], _split_w_in(w_in[0]))
    bias = _bias_per_offset(rel_bias)
    outs, stats = [], []
    for g, (zg, col0) in enumerate(((z_nat[:, None], NAT_QKV), (z_d4, 0), (z_d16, 0))):
        o, st = _attn_group(zg, bias, g, col0, *ATTN_STEP[g])
        outs.append(o)
        stats.append(st)
    return _epilogue(x, z_nat, outs, stats, w_pool[0], pool_scale[0][None, :],
                     w_proj_a[0], w_proj_b[0], w_out[0], 0.5 * b_gate[0], final_gain[None, :])
```

```python
import functools
import math

import jax
import jax.numpy as jnp
import numpy as np
from jax import lax
from jax.experimental import pallas as pl
from jax.experimental.pallas import tpu as pltpu

D_MODEL = 1024
BRANCH_W = D_MODEL // 2
POOL_WINDOWS = (2, 4, 8, 16)
POOL_GROUP_W = BRANCH_W // len(POOL_WINDOWS)
DILATED_GROUPS = ((128, 1), (512, 4), (2048, 16))
N_ATT_GROUPS = len(DILATED_GROUPS)
HEAD_DIM = 64
HEADS_PER_GROUP = BRANCH_W // HEAD_DIM
QKV_W = N_ATT_GROUPS * BRANCH_W
NUM_BUCKETS = 32
MAX_DISTANCE = 1024
IN_W = 2 * BRANCH_W + 3 * QKV_W + BRANCH_W + 2 * D_MODEL
RMS_EPS = 1e-6
NEG_INF = -1e30
LOG2_E = math.log2(math.e)
SCORE_SCALE = LOG2_E / math.sqrt(HEAD_DIM)

COL_Q = 2 * BRANCH_W
COL_K = COL_Q + QKV_W
COL_V = COL_K + QKV_W
COL_ATT_GATE = COL_V + QKV_W
COL_MERGE = COL_ATT_GATE + BRANCH_W

NAT_ATT_GATE = 6
NAT_QKV = 7
NAT_W = 10 * BRANCH_W
QKV_G_W = 3 * BRANCH_W

N_SIDE = 64
LANES = 128
Q_SUB = 128
K_SUB = Q_SUB + 2 * N_SIDE
ATTN_STEP = ((1, 2048), (1, 2048), (4, 512))
HALO_ROWS = 16
POOL_ROWS = 128
POOL_K = POOL_ROWS + 2 * HALO_ROWS

IN_PROJ_ROWS = 512
IN_PROJ_COLS = 512
EPI_ROWS = 512
V7X_VMEM_BYTES = 64 * 1024 * 1024
VMEM_LIMIT = V7X_VMEM_BYTES * 7 // 8

_BF16 = jnp.bfloat16
_F32 = jnp.float32


def _in_proj_kernel(x_ref, g_ref, w_ref, zn_ref, z4_ref, z16_ref,
                    hf_ref, hn_ref, h4_ref, h16_ref):
    tm = x_ref.shape[0]
    x = x_ref[...]
    r = lax.rsqrt(jnp.mean(x * x, axis=-1, keepdims=True) + RMS_EPS)
    hf = (x * r) * g_ref[...]
    n_slab = D_MODEL // LANES
    for c in range(n_slab):
        hf_ref[c] = hf[:, c * LANES:(c + 1) * LANES]
    hn_ref[...] = hf.astype(_BF16)
    for c in range(NAT_W // IN_PROJ_COLS):
        cs = slice(c * IN_PROJ_COLS, (c + 1) * IN_PROJ_COLS)
        zn_ref[:, cs] = jnp.dot(hn_ref[...], w_ref[:, cs],
                                preferred_element_type=_F32).astype(_BF16)
    for dil, h_ref, col0, z_ref in ((4, h4_ref, NAT_W, z4_ref),
                                    (16, h16_ref, NAT_W + QKV_G_W, z16_ref)):
        rows = tm // dil
        for res in range(dil):
            for c in range(n_slab):
                h_ref[res * rows:(res + 1) * rows, c * LANES:(c + 1) * LANES] = (
                    hf_ref[c, pl.ds(res, rows, stride=dil), :].astype(_BF16))
        for c in range(QKV_G_W // IN_PROJ_COLS):
            cs = slice(c * IN_PROJ_COLS, (c + 1) * IN_PROJ_COLS)
            zc = jnp.dot(h_ref[...], w_ref[:, col0 + cs.start:col0 + cs.stop],
                         preferred_element_type=_F32).astype(_BF16)
            for res in range(dil):
                z_ref[res, :, cs] = zc[res * rows:(res + 1) * rows]


def _in_proj(x, gain, w_all):
    b, s, _ = x.shape
    tm = IN_PROJ_ROWS
    return pl.pallas_call(
        _in_proj_kernel,
        grid=(b, s // tm),
        in_specs=[
            pl.BlockSpec((None, tm, D_MODEL), lambda bi, i: (bi, i, 0)),
            pl.BlockSpec((1, D_MODEL), lambda bi, i: (0, 0)),
            pl.BlockSpec(w_all.shape, lambda bi, i: (0, 0), pipeline_mode=pl.Buffered(1)),
        ],
        out_specs=[
            pl.BlockSpec((None, tm, NAT_W), lambda bi, i: (bi, i, 0)),
            pl.BlockSpec((None, 4, tm // 4, QKV_G_W), lambda bi, i: (bi, 0, i, 0)),
            pl.BlockSpec((None, 16, tm // 16, QKV_G_W), lambda bi, i: (bi, 0, i, 0)),
        ],
        out_shape=[
            jax.ShapeDtypeStruct((b, s, NAT_W), _BF16),
            jax.ShapeDtypeStruct((b, 4, s // 4, QKV_G_W), _BF16),
            jax.ShapeDtypeStruct((b, 16, s // 16, QKV_G_W), _BF16),
        ],
        scratch_shapes=[pltpu.VMEM((D_MODEL // LANES, tm, LANES), _F32),
                        pltpu.VMEM((tm, D_MODEL), _BF16),
                        pltpu.VMEM((tm, D_MODEL), _BF16), pltpu.VMEM((tm, D_MODEL), _BF16)],
        compiler_params=pltpu.CompilerParams(
            dimension_semantics=("arbitrary", "arbitrary"), vmem_limit_bytes=VMEM_LIMIT),
        name="in_proj",
    )(x, gain, w_all)


def _attn_kernel(q_ref, kp_ref, km_ref, kn_ref, vp_ref, vm_ref, vn_ref, per_off_ref,
                 o_ref, st_ref, bias_ref, *, n_steps):
    n_res, q_step = q_ref.shape[0], q_ref.shape[1]
    j = pl.program_id(2)

    @pl.when(jnp.logical_and(jnp.logical_and(pl.program_id(0) == 0, pl.program_id(1) == 0),
                             j == 0))
    def _build_bias():
        key = lax.broadcasted_iota(jnp.int32, (Q_SUB, K_SUB), 1)
        for h in range(HEADS_PER_GROUP):
            rows = slice((h % 2) * Q_SUB, (h % 2 + 1) * Q_SUB)
            band = pltpu.roll(jnp.broadcast_to(per_off_ref[h:h + 1, :], (Q_SUB, K_SUB)),
                              0, 1, stride=1, stride_axis=0)
            bias_ref[0, h // 2, rows, :] = band
            bias_ref[1, h // 2, rows, :] = jnp.where(key < N_SIDE, NEG_INF, band)
            bias_ref[2, h // 2, rows, :] = jnp.where(key >= K_SUB - N_SIDE, NEG_INF, band)

    lane = lax.broadcasted_iota(jnp.int32, (Q_SUB, LANES), 1)
    low_head = lane < HEAD_DIM
    n_sub = q_step // Q_SUB
    ones = jnp.ones((K_SUB, LANES), _BF16)

    def window(prev_ref, main_ref, next_ref, res, i, cs):
        if i == 0:
            return jnp.concatenate(
                [prev_ref[res, :, cs], main_ref[res, 0:K_SUB - N_SIDE, cs]], axis=0)
        if i == n_sub - 1:
            return jnp.concatenate(
                [main_ref[res, q_step - (K_SUB - N_SIDE):q_step, cs], next_ref[res, :, cs]], axis=0)
        return main_ref[res, i * Q_SUB - N_SIDE:i * Q_SUB - N_SIDE + K_SUB, cs]

    for res in range(n_res):
        for i in range(n_sub):
            r0 = i * Q_SUB
            if i == 0:
                var = jnp.where(j == 0, 1, 0)
            elif i == n_sub - 1:
                var = jnp.where(j == n_steps - 1, 2, 0)
            else:
                var = 0
            stats = jnp.zeros((Q_SUB, LANES), _F32)
            for hp in range(HEADS_PER_GROUP // 2):
                cs = slice(hp * LANES, (hp + 1) * LANES)
                q2 = q_ref[res, r0:r0 + Q_SUB, cs]
                k2 = window(kp_ref, km_ref, kn_ref, res, i, cs)
                v2 = jnp.concatenate([window(vp_ref, vm_ref, vn_ref, res, i, cs), ones], axis=1)
                zero = jnp.zeros_like(q2)
                qs = jnp.concatenate([jnp.where(low_head, q2, zero),
                                      jnp.where(low_head, zero, q2)], axis=0)
                s = lax.dot_general(qs, k2, (((1,), (1,)), ((), ())),
                                    preferred_element_type=_F32)
                s = s + bias_ref[var, hp]
                m = jnp.max(s, axis=-1, keepdims=True)
                p = jnp.exp2(s - m)
                ol = jnp.dot(p.astype(_BF16), v2, preferred_element_type=_F32)
                o, l = ol[:, :LANES], ol[:, LANES:]
                o_ref[res, r0:r0 + Q_SUB, cs] = (
                    jnp.where(low_head, o[:Q_SUB], o[Q_SUB:]).astype(_BF16))
                for a in range(2):
                    h = 2 * hp + a
                    rows = slice(a * Q_SUB, (a + 1) * Q_SUB)
                    stats = jnp.where(lane == h, m[rows], stats)
                    stats = jnp.where(lane == HEADS_PER_GROUP + h, l[rows], stats)
            st_ref[res, r0:r0 + Q_SUB, :] = stats


def _attn_group(zg, per_off, g, col0, n_res, q_step):
    b, dil, sp, _ = zg.shape
    assert dil % n_res == 0 and sp % q_step == 0 and q_step >= 2 * Q_SUB
    n_steps = sp // q_step
    hb = q_step // N_SIDE
    last_hb = sp // N_SIDE - 1

    def main(col, width=BRANCH_W):
        return pl.BlockSpec((None, n_res, q_step, width), lambda bi, r, j: (bi, r, j, col))

    def prev(col):
        return pl.BlockSpec((None, n_res, N_SIDE, BRANCH_W),
                            lambda bi, r, j: (bi, r, jnp.maximum(j * hb - 1, 0), col))

    def nxt(col):
        return pl.BlockSpec((None, n_res, N_SIDE, BRANCH_W),
                            lambda bi, r, j: (bi, r, jnp.minimum((j + 1) * hb, last_hb), col))

    qc, kc, vc = col0, col0 + 1, col0 + 2
    return pl.pallas_call(
        functools.partial(_attn_kernel, n_steps=n_steps),
        grid=(b, dil // n_res, n_steps),
        in_specs=[main(qc), prev(kc), main(kc), nxt(kc), prev(vc), main(vc), nxt(vc),
                  pl.BlockSpec((None,) + per_off.shape[1:], lambda bi, r, j: (g, 0, 0))],
        out_specs=[main(0), main(0, LANES)],
        out_shape=[
            jax.ShapeDtypeStruct((b, dil, sp, BRANCH_W), _BF16),
            jax.ShapeDtypeStruct((b, dil, sp, LANES), _F32),
        ],
        scratch_shapes=[pltpu.VMEM((3, HEADS_PER_GROUP // 2, 2 * Q_SUB, K_SUB), _F32)],
        compiler_params=pltpu.CompilerParams(
            dimension_semantics=("arbitrary", "arbitrary", "arbitrary"),
            vmem_limit_bytes=VMEM_LIMIT),
        name=f"attn_d{dil}",
    )(zg, zg, zg, zg, zg, zg, zg, per_off)


def _t5_bucket(rel):
    nb = NUM_BUCKETS // 2
    ret = (rel > 0).astype(jnp.int32) * nb
    n = jnp.abs(rel)
    max_exact = nb // 2
    nf = jnp.maximum(n, 1).astype(_F32)
    large = max_exact + (jnp.log(nf / max_exact) / math.log(MAX_DISTANCE / max_exact)
                         * (nb - max_exact)).astype(jnp.int32)
    large = jnp.minimum(large, nb - 1)
    return ret + jnp.where(n < max_exact, n, large)


def _bias_per_offset(rel_bias):
    n_off = 2 * N_SIDE + 1
    dils = jnp.asarray([dil for _, dil in DILATED_GROUPS], jnp.int32)[:, None]
    offs = jnp.arange(-N_SIDE, N_SIDE + 1, dtype=jnp.int32)[None, :] * dils
    by_group = rel_bias.reshape(NUM_BUCKETS, N_ATT_GROUPS, HEADS_PER_GROUP)
    per_off = by_group[_t5_bucket(offs), jnp.arange(N_ATT_GROUPS)[:, None]]
    per_off = jnp.swapaxes(per_off, 1, 2).astype(_F32) * LOG2_E
    pad = jnp.full((N_ATT_GROUPS, HEADS_PER_GROUP, K_SUB - n_off), NEG_INF, _F32)
    return jnp.concatenate([per_off, pad], axis=2)


def _silu_of_half(h):
    return h + h * jnp.tanh(h)


def _epilogue_kernel(x_ref, za_ref, zprev_ref, znext_ref, zga_ref, zgb_ref, zbg_ref,
                     o1_ref, o4_ref, o16_ref, s1_ref, s4_ref, s16_ref,
                     icount_in, pscale_ref, bgate_ref, fgain_ref,
                     pmat_in, wpool_in, wpa_in, wpb_in, wout_in, expand_in,
                     out_ref,
                     ext_ref, o4n_ref, o16n_ref, s4n_ref, s16n_ref, icount_ref,
                     pmat_ref, wpool_ref, wpa_ref, wpb_ref, wout_ref, expand_ref, *, seq_len):
    tm = x_ref.shape[0]
    p0 = pl.program_id(1) * tm

    @pl.when(jnp.logical_and(pl.program_id(0) == 0, pl.program_id(1) == 0))
    def _stage_weights():
        for src, dst in ((pmat_in, pmat_ref), (wpa_in, wpa_ref), (wpb_in, wpb_ref),
                         (expand_in, expand_ref)):
            dst[...] = src[...].astype(_BF16)
        wout_ref[...] = (0.5 * wout_in[...]).astype(_BF16)
        wpool_ref[...] = jnp.zeros_like(wpool_ref)
        for gi in range(len(POOL_WINDOWS)):
            d0 = (gi % 2) * POOL_GROUP_W
            wpool_ref[gi // 2, d0:d0 + POOL_GROUP_W, d0:d0 + POOL_GROUP_W] = (
                wpool_in[gi].astype(_BF16))
        icount_ref[...] = jnp.broadcast_to(icount_in[0, 0:1, :], icount_ref.shape)

    icount_ref[0:HALO_ROWS] = jnp.where(p0 == 0, icount_in[1], icount_in[0])
    icount_ref[tm - HALO_ROWS:tm] = jnp.where(p0 + tm == seq_len, icount_in[2], icount_in[0])

    has_prev = (p0 > 0).astype(_BF16)
    has_next = (p0 + tm < seq_len).astype(_BF16)
    ext_ref[0:HALO_ROWS] = zprev_ref[...] * has_prev
    ext_ref[HALO_ROWS:HALO_ROWS + tm] = za_ref[:, 0:BRANCH_W]
    ext_ref[HALO_ROWS + tm:] = znext_ref[...] * has_next

    for dil, o_ref, on_ref, s_ref, sn_ref in ((4, o4_ref, o4n_ref, s4_ref, s4n_ref),
                                              (16, o16_ref, o16n_ref, s16_ref, s16n_ref)):
        rows = tm // dil
        for res in range(dil):
            o_res = o_ref[res].astype(_F32)
            for c in range(BRANCH_W // LANES):
                on_ref[c, pl.ds(res, rows, stride=dil), :] = o_res[:, c * LANES:(c + 1) * LANES]
            sn_ref[pl.ds(res, rows, stride=dil), :] = s_ref[res]

    n_chunks = tm // POOL_ROWS
    edge = [jnp.where(p0 == 0, 1, 0)] + [0] * (n_chunks - 2) + [jnp.where(p0 + tm == seq_len, 2, 0)]
    pooled = []
    for gi in range(len(POOL_WINDOWS)):
        cs = slice(gi * POOL_GROUP_W, (gi + 1) * POOL_GROUP_W)
        win = jnp.concatenate(
            [jnp.dot(pmat_ref[edge[ci], gi], ext_ref[ci * POOL_ROWS:ci * POOL_ROWS + POOL_K, cs],
                     preferred_element_type=_F32) for ci in range(n_chunks)], axis=0)
        pooled.append((win * icount_ref[:, cs]).astype(_BF16))
    mixed = [jnp.dot(jnp.concatenate(pooled[2 * pr:2 * pr + 2], axis=-1), wpool_ref[pr],
                     preferred_element_type=_F32) for pr in range(len(POOL_WINDOWS) // 2)]
    mixed = jnp.concatenate(mixed, axis=-1) * pscale_ref[...]
    ya_in = mixed * _silu_of_half(za_ref[:, BRANCH_W:].astype(_F32))
    y_a = jnp.dot(ya_in.astype(_BF16), wpa_ref[...], preferred_element_type=_F32)

    st = (s1_ref[...], s4n_ref[...], s16n_ref[...])
    mx = jnp.maximum(jnp.maximum(st[0], st[1]), st[2])
    es = [jnp.exp2(t - mx) for t in st]
    ls = [pltpu.roll(t, LANES - HEADS_PER_GROUP, 1) for t in st]
    head_lane = lax.broadcasted_iota(jnp.int32, (tm, LANES), 1) < HEADS_PER_GROUP
    den = jnp.where(head_lane, es[0] * ls[0] + es[1] * ls[1] + es[2] * ls[2], 1.0)
    inv = 1.0 / den
    packed = jnp.zeros((tm, LANES), _F32)
    for g, e in enumerate(es):
        wgt = jnp.where(head_lane, e * inv, 0.0)
        hi_part = wgt.astype(_BF16).astype(_F32)
        for part, lane0 in ((hi_part, HEADS_PER_GROUP * g),
                            (wgt - hi_part, HEADS_PER_GROUP * (N_ATT_GROUPS + g))):
            packed = packed + (pltpu.roll(part, lane0, 1) if lane0 else part)
    wide = jnp.dot(packed.astype(_BF16), expand_ref[...], preferred_element_type=_F32)
    ogs = (o1_ref[...].astype(_F32),
           jnp.concatenate([o4n_ref[c] for c in range(BRANCH_W // LANES)], axis=-1),
           jnp.concatenate([o16n_ref[c] for c in range(BRANCH_W // LANES)], axis=-1))
    att = sum(wide[:, g * BRANCH_W:(g + 1) * BRANCH_W] * og for g, og in enumerate(ogs))
    yb_in = att * _silu_of_half(zbg_ref[...].astype(_F32))
    y_b = jnp.dot(yb_in.astype(_BF16), wpb_ref[...], preferred_element_type=_F32)

    t_a = jnp.tanh(zga_ref[...].astype(_F32) + 0.5 * bgate_ref[0:1, :])
    t_b = jnp.tanh(zgb_ref[...].astype(_F32) + 0.5 * bgate_ref[1:2, :])
    merged2 = (y_a + y_b) + (t_a * y_a + t_b * y_b)
    xo = x_ref[...] + jnp.dot(merged2.astype(_BF16), wout_ref[...], preferred_element_type=_F32)
    r = lax.rsqrt(jnp.mean(xo * xo, axis=-1, keepdims=True) + RMS_EPS)
    out_ref[...] = (xo * r) * fgain_ref[...]


def _window_counts(seq_len, pos):
    pos = np.asarray(pos)[:, None]
    half = np.asarray([w // 2 for w in POOL_WINDOWS])[None, :]
    return np.minimum(pos + half - 1, seq_len - 1) - np.maximum(pos - half, 0) + 1


def _pool_matrices(seq_len):
    r = np.arange(POOL_ROWS)[:, None]
    c = np.arange(POOL_K)[None, :] - HALO_ROWS
    mid = seq_len // 2
    starts = (mid, 0, seq_len - POOL_ROWS)
    out = np.zeros((3, len(POOL_WINDOWS), POOL_ROWS, POOL_K), np.float32)
    for v, start in enumerate(starts):
        counts = _window_counts(seq_len, range(start, start + POOL_ROWS))
        for gi, w in enumerate(POOL_WINDOWS):
            window = (c - r >= -(w // 2)) & (c - r <= w // 2 - 1)
            out[v, gi] = window - counts[:, gi:gi + 1] * (c == r)
    return out


def _inv_window_counts(seq_len):
    mid = seq_len // 2
    rows = lambda start: np.repeat(
        1.0 / _window_counts(seq_len, range(start, start + HALO_ROWS)), POOL_GROUP_W, axis=1)
    return np.stack([rows(mid), rows(0), rows(seq_len - HALO_ROWS)]).astype(np.float32)


def _expand_matrix():
    row = np.arange(LANES)[:, None]
    col = np.arange(N_ATT_GROUPS * BRANCH_W)[None, :]
    n_packed = N_ATT_GROUPS * HEADS_PER_GROUP
    hit = ((row < 2 * n_packed)
           & ((row % n_packed) // HEADS_PER_GROUP == col // BRANCH_W)
           & (row % HEADS_PER_GROUP == (col % BRANCH_W) // HEAD_DIM))
    return hit.astype(np.float32)


def _epilogue(x, z_nat, outs, stats, w_pool, pool_scale, w_pa, w_pb, w_out, b_gate, final_gain):
    b, s, _ = x.shape
    tm = EPI_ROWS
    halo_per_tile = tm // HALO_ROWS
    last_halo = s // HALO_ROWS - 1
    pair_w = 2 * POOL_GROUP_W
    staged = ((jnp.asarray(_pool_matrices(s)[None], _BF16), None),
              (w_pool, (len(POOL_WINDOWS) // 2, pair_w, pair_w)),
              (w_pa, None), (w_pb, None), (w_out, None),
              (jnp.asarray(_expand_matrix()[None], _BF16), None))

    def full(a):
        return pl.BlockSpec(a.shape, lambda bi, i: (0,) * a.ndim)

    def layer0(a, **kw):
        return pl.BlockSpec((None,) + a.shape[1:], lambda bi, i: (0,) * a.ndim, **kw)

    def once(a):
        return layer0(a, pipeline_mode=pl.Buffered(1))

    def row(w, col):
        return pl.BlockSpec((None, tm, w), lambda bi, i: (bi, i, col))

    def by_residue(a):
        dil, w = a.shape[1], a.shape[3]
        return pl.BlockSpec((None, dil, tm // dil, w), lambda bi, i: (bi, 0, i, 0))

    in_specs = [
        row(D_MODEL, 0),
        row(2 * BRANCH_W, 0),
        pl.BlockSpec((None, HALO_ROWS, BRANCH_W),
                     lambda bi, i: (bi, jnp.maximum(i * halo_per_tile - 1, 0), 0)),
        pl.BlockSpec((None, HALO_ROWS, BRANCH_W),
                     lambda bi, i: (bi, jnp.minimum((i + 1) * halo_per_tile, last_halo), 0)),
        row(D_MODEL, 1), row(D_MODEL, 2),
        row(BRANCH_W, NAT_ATT_GATE),
        pl.BlockSpec((None, None, tm, BRANCH_W), lambda bi, i: (bi, 0, i, 0)),
        by_residue(outs[1]), by_residue(outs[2]),
        pl.BlockSpec((None, None, tm, LANES), lambda bi, i: (bi, 0, i, 0)),
        by_residue(stats[1]), by_residue(stats[2]),
        pl.BlockSpec((3, HALO_ROWS, BRANCH_W), lambda bi, i: (0, 0, 0)),
        full(pool_scale), layer0(b_gate), full(final_gain),
    ] + [once(a) for a, _ in staged]
    return pl.pallas_call(
        functools.partial(_epilogue_kernel, seq_len=s),
        grid=(b, s // tm),
        in_specs=in_specs,
        out_specs=pl.BlockSpec((None, tm, D_MODEL), lambda bi, i: (bi, i, 0)),
        out_shape=jax.ShapeDtypeStruct((b, s, D_MODEL), _F32),
        scratch_shapes=[pltpu.VMEM((tm + 2 * HALO_ROWS, BRANCH_W), _BF16),
                        pltpu.VMEM((BRANCH_W // LANES, tm, LANES), _F32),
                        pltpu.VMEM((BRANCH_W // LANES, tm, LANES), _F32),
                        pltpu.VMEM((tm, LANES), _F32), pltpu.VMEM((tm, LANES), _F32),
                        pltpu.VMEM((tm, BRANCH_W), _F32)]
        + [pltpu.VMEM(shape or a.shape[1:], _BF16) for a, shape in staged],
        compiler_params=pltpu.CompilerParams(
            dimension_semantics=("arbitrary", "arbitrary"), vmem_limit_bytes=VMEM_LIMIT),
        name="epilogue",
    )(x, z_nat, z_nat, z_nat, z_nat, z_nat, z_nat, *outs, *stats,
      jnp.asarray(_inv_window_counts(s)), pool_scale, b_gate, final_gain,
      *[a for a, _ in staged])


def _split_w_in(w):
    def qkv(g):
        q, k, v = (w[:, c + g * BRANCH_W:c + (g + 1) * BRANCH_W] for c in (COL_Q, COL_K, COL_V))
        return [q * SCORE_SCALE, k, v]
    nat = [w[:, :BRANCH_W], 0.5 * w[:, BRANCH_W:COL_Q], 0.5 * w[:, COL_MERGE:],
           0.5 * w[:, COL_ATT_GATE:COL_MERGE]] + qkv(0)
    return jnp.concatenate(nat + qkv(1) + qkv(2), axis=1).astype(_BF16)


def kernel(x, norm_gain, w_in, b_gate, rel_bias, w_pool, pool_scale, w_proj_a, w_proj_b,
           w_out, final_gain):
    b, s, d = x.shape
    assert d == D_MODEL and norm_gain.shape[0] == 1, "single-layer block of width D_MODEL"
    assert s % EPI_ROWS == 0 and s % IN_PROJ_ROWS == 0
    assert [dil for _, dil in DILATED_GROUPS] == [1, 4, 16]
    z_nat, z_d4, z_d16 = _in_proj(x, norm_gain, _split_w_in(w_in[0]))
    bias = _bias_per_offset(rel_bias)
    outs, stats = [], []
    for g, (zg, col0) in enumerate(((z_nat[:, None], NAT_QKV), (z_d4, 0), (z_d16, 0))):
        o, st = _attn_group(zg, bias, g, col0, *ATTN_STEP[g])
        outs.append(o)
        stats.append(st)
    return _epilogue(x, z_nat, outs, stats, w_pool, pool_scale, w_proj_a, w_proj_b, w_out,
                     b_gate, final_gain[None, :])
```

```python
import functools
import math

import jax
import jax.numpy as jnp
import numpy as np
from jax import lax
from jax.experimental import pallas as pl
from jax.experimental.pallas import tpu as pltpu

D_MODEL = 1024
BRANCH_W = D_MODEL // 2
POOL_WINDOWS = (2, 4, 8, 16)
POOL_GROUP_W = BRANCH_W // len(POOL_WINDOWS)
DILATED_GROUPS = ((128, 1), (512, 4), (2048, 16))
N_ATT_GROUPS = len(DILATED_GROUPS)
HEAD_DIM = 64
HEADS_PER_GROUP = BRANCH_W // HEAD_DIM
QKV_W = N_ATT_GROUPS * BRANCH_W
NUM_BUCKETS = 32
MAX_DISTANCE = 1024
IN_W = 2 * BRANCH_W + 3 * QKV_W + BRANCH_W + 2 * D_MODEL
RMS_EPS = 1e-6
NEG_INF = -1e30
LOG2_E = math.log2(math.e)
SCORE_SCALE = LOG2_E / math.sqrt(HEAD_DIM)

COL_Q = 2 * BRANCH_W
COL_K = COL_Q + QKV_W
COL_V = COL_K + QKV_W
COL_ATT_GATE = COL_V + QKV_W
COL_MERGE = COL_ATT_GATE + BRANCH_W

NAT_ATT_GATE = 6
NAT_QKV = 7
NAT_SRC_BLOCKS = (0, 1, *range(COL_MERGE // BRANCH_W, IN_W // BRANCH_W), COL_ATT_GATE // BRANCH_W,
                  COL_Q // BRANCH_W, COL_K // BRANCH_W, COL_V // BRANCH_W)
NAT_W = len(NAT_SRC_BLOCKS) * BRANCH_W
QKV_G_W = 3 * BRANCH_W

N_SIDE = 64
LANES = 128
Q_SUB = 128
K_SUB = Q_SUB + 2 * N_SIDE
ATTN_STEP = ((1, 2048), (1, 2048), (4, 512))
HALO_ROWS = 16
POOL_ROWS = 128
POOL_K = POOL_ROWS + 2 * HALO_ROWS

IN_PROJ_ROWS = 512
EPI_ROWS = 512
V7X_VMEM_BYTES = 64 * 1024 * 1024
VMEM_LIMIT = V7X_VMEM_BYTES * 7 // 8

_BF16 = jnp.bfloat16
_F32 = jnp.float32


def _in_proj_kernel(x_ref, g_ref, w_ref, zn_ref, z4_ref, z16_ref,
                    hf_ref, hn_ref, h4_ref, h16_ref):
    tm = x_ref.shape[0]
    x = x_ref[...]
    r = lax.rsqrt(jnp.mean(x * x, axis=-1, keepdims=True) + RMS_EPS)
    hf = (x * r) * g_ref[...]
    n_slab = D_MODEL // LANES
    for c in range(n_slab):
        hf_ref[c] = hf[:, c * LANES:(c + 1) * LANES]
    hn_ref[...] = hf.astype(_BF16)

    def project(h_ref, src_block):
        cs = slice(src_block * BRANCH_W, (src_block + 1) * BRANCH_W)
        return jnp.dot(h_ref[...], w_ref[:, cs], preferred_element_type=_F32).astype(_BF16)

    for dst_block, src_block in enumerate(NAT_SRC_BLOCKS):
        zn_ref[:, dst_block * BRANCH_W:(dst_block + 1) * BRANCH_W] = project(hn_ref, src_block)
    for g, dil, h_ref, z_ref in ((1, 4, h4_ref, z4_ref), (2, 16, h16_ref, z16_ref)):
        rows = tm // dil
        for res in range(dil):
            for c in range(n_slab):
                h_ref[res * rows:(res + 1) * rows, c * LANES:(c + 1) * LANES] = (
                    hf_ref[c, pl.ds(res, rows, stride=dil), :].astype(_BF16))
        for dst_block, col in enumerate((COL_Q, COL_K, COL_V)):
            zc = project(h_ref, col // BRANCH_W + g)
            for res in range(dil):
                z_ref[res, :, dst_block * BRANCH_W:(dst_block + 1) * BRANCH_W] = (
                    zc[res * rows:(res + 1) * rows])


def _in_proj(x, gain, w_all):
    b, s, _ = x.shape
    tm = IN_PROJ_ROWS
    return pl.pallas_call(
        _in_proj_kernel,
        grid=(b, s // tm),
        in_specs=[
            pl.BlockSpec((None, tm, D_MODEL), lambda bi, i: (bi, i, 0)),
            pl.BlockSpec((1, D_MODEL), lambda bi, i: (0, 0)),
            pl.BlockSpec(w_all.shape, lambda bi, i: (0, 0), pipeline_mode=pl.Buffered(1)),
        ],
        out_specs=[
            pl.BlockSpec((None, tm, NAT_W), lambda bi, i: (bi, i, 0)),
            pl.BlockSpec((None, 4, tm // 4, QKV_G_W), lambda bi, i: (bi, 0, i, 0)),
            pl.BlockSpec((None, 16, tm // 16, QKV_G_W), lambda bi, i: (bi, 0, i, 0)),
        ],
        out_shape=[
            jax.ShapeDtypeStruct((b, s, NAT_W), _BF16),
            jax.ShapeDtypeStruct((b, 4, s // 4, QKV_G_W), _BF16),
            jax.ShapeDtypeStruct((b, 16, s // 16, QKV_G_W), _BF16),
        ],
        scratch_shapes=[pltpu.VMEM((D_MODEL // LANES, tm, LANES), _F32),
                        pltpu.VMEM((tm, D_MODEL), _BF16),
                        pltpu.VMEM((tm, D_MODEL), _BF16), pltpu.VMEM((tm, D_MODEL), _BF16)],
        compiler_params=pltpu.CompilerParams(
            dimension_semantics=("arbitrary", "arbitrary"), vmem_limit_bytes=VMEM_LIMIT),
        name="in_proj",
    )(x, gain, w_all)


def _attn_kernel(q_ref, kp_ref, km_ref, kn_ref, vp_ref, vm_ref, vn_ref, per_off_ref,
                 o_ref, st_ref, bias_ref, *, n_steps):
    n_res, q_step = q_ref.shape[0], q_ref.shape[1]
    j = pl.program_id(2)

    @pl.when(jnp.logical_and(jnp.logical_and(pl.program_id(0) == 0, pl.program_id(1) == 0),
                             j == 0))
    def _build_bias():
        key = lax.broadcasted_iota(jnp.int32, (Q_SUB, K_SUB), 1)
        for h in range(HEADS_PER_GROUP):
            rows = slice((h % 2) * Q_SUB, (h % 2 + 1) * Q_SUB)
            band = pltpu.roll(jnp.broadcast_to(per_off_ref[h:h + 1, :], (Q_SUB, K_SUB)),
                              0, 1, stride=1, stride_axis=0)
            bias_ref[0, h // 2, rows, :] = band
            bias_ref[1, h // 2, rows, :] = jnp.where(key < N_SIDE, NEG_INF, band)
            bias_ref[2, h // 2, rows, :] = jnp.where(key >= K_SUB - N_SIDE, NEG_INF, band)

    lane = lax.broadcasted_iota(jnp.int32, (Q_SUB, LANES), 1)
    low_head = lane < HEAD_DIM
    n_sub = q_step // Q_SUB
    ones = jnp.ones((K_SUB, LANES), _BF16)

    def window(prev_ref, main_ref, next_ref, res, i, cs):
        if i == 0:
            return jnp.concatenate(
                [prev_ref[res, :, cs], main_ref[res, 0:K_SUB - N_SIDE, cs]], axis=0)
        if i == n_sub - 1:
            return jnp.concatenate(
                [main_ref[res, q_step - (K_SUB - N_SIDE):q_step, cs], next_ref[res, :, cs]], axis=0)
        return main_ref[res, i * Q_SUB - N_SIDE:i * Q_SUB - N_SIDE + K_SUB, cs]

    for res in range(n_res):
        for i in range(n_sub):
            r0 = i * Q_SUB
            if i == 0:
                var = jnp.where(j == 0, 1, 0)
            elif i == n_sub - 1:
                var = jnp.where(j == n_steps - 1, 2, 0)
            else:
                var = 0
            stats = jnp.zeros((Q_SUB, LANES), _F32)
            for hp in range(HEADS_PER_GROUP // 2):
                cs = slice(hp * LANES, (hp + 1) * LANES)
                q2 = q_ref[res, r0:r0 + Q_SUB, cs]
                k2 = window(kp_ref, km_ref, kn_ref, res, i, cs)
                v2 = jnp.concatenate([window(vp_ref, vm_ref, vn_ref, res, i, cs), ones], axis=1)
                zero = jnp.zeros_like(q2)
                qs = jnp.concatenate([jnp.where(low_head, q2, zero),
                                      jnp.where(low_head, zero, q2)], axis=0)
                s = lax.dot_general(qs, k2, (((1,), (1,)), ((), ())),
                                    preferred_element_type=_F32)
                s = s + bias_ref[var, hp]
                m = jnp.max(s, axis=-1, keepdims=True)
                p = jnp.exp2(s - m)
                ol = jnp.dot(p.astype(_BF16), v2, preferred_element_type=_F32)
                o, l = ol[:, :LANES], ol[:, LANES:]
                o_ref[res, r0:r0 + Q_SUB, cs] = (
                    jnp.where(low_head, o[:Q_SUB], o[Q_SUB:]).astype(_BF16))
                for a in range(2):
                    h = 2 * hp + a
                    rows = slice(a * Q_SUB, (a + 1) * Q_SUB)
                    stats = jnp.where(lane == h, m[rows], stats)
                    stats = jnp.where(lane == HEADS_PER_GROUP + h, l[rows], stats)
            st_ref[res, r0:r0 + Q_SUB, :] = stats


def _attn_group(zg, per_off, g, col0, n_res, q_step):
    b, dil, sp, _ = zg.shape
    assert dil % n_res == 0 and sp % q_step == 0 and q_step >= 2 * Q_SUB
    n_steps = sp // q_step
    hb = q_step // N_SIDE
    last_hb = sp // N_SIDE - 1

    def main(col, width=BRANCH_W):
        return pl.BlockSpec((None, n_res, q_step, width), lambda bi, r, j: (bi, r, j, col))

    def prev(col):
        return pl.BlockSpec((None, n_res, N_SIDE, BRANCH_W),
                            lambda bi, r, j: (bi, r, jnp.maximum(j * hb - 1, 0), col))

    def nxt(col):
        return pl.BlockSpec((None, n_res, N_SIDE, BRANCH_W),
                            lambda bi, r, j: (bi, r, jnp.minimum((j + 1) * hb, last_hb), col))

    qc, kc, vc = col0, col0 + 1, col0 + 2
    return pl.pallas_call(
        functools.partial(_attn_kernel, n_steps=n_steps),
        grid=(b, dil // n_res, n_steps),
        in_specs=[main(qc), prev(kc), main(kc), nxt(kc), prev(vc), main(vc), nxt(vc),
                  pl.BlockSpec((None,) + per_off.shape[1:], lambda bi, r, j: (g, 0, 0))],
        out_specs=[main(0), main(0, LANES)],
        out_shape=[
            jax.ShapeDtypeStruct((b, dil, sp, BRANCH_W), _BF16),
            jax.ShapeDtypeStruct((b, dil, sp, LANES), _F32),
        ],
        scratch_shapes=[pltpu.VMEM((3, HEADS_PER_GROUP // 2, 2 * Q_SUB, K_SUB), _F32)],
        compiler_params=pltpu.CompilerParams(
            dimension_semantics=("arbitrary", "arbitrary", "arbitrary"),
            vmem_limit_bytes=VMEM_LIMIT),
        name=f"attn_d{dil}",
    )(zg, zg, zg, zg, zg, zg, zg, per_off)


def _t5_bucket(rel):
    nb = NUM_BUCKETS // 2
    ret = (rel > 0).astype(jnp.int32) * nb
    n = jnp.abs(rel)
    max_exact = nb // 2
    nf = jnp.maximum(n, 1).astype(_F32)
    large = max_exact + (jnp.log(nf / max_exact) / math.log(MAX_DISTANCE / max_exact)
                         * (nb - max_exact)).astype(jnp.int32)
    large = jnp.minimum(large, nb - 1)
    return ret + jnp.where(n < max_exact, n, large)


def _bias_per_offset(rel_bias):
    n_off = 2 * N_SIDE + 1
    dils = jnp.asarray([dil for _, dil in DILATED_GROUPS], jnp.int32)[:, None]
    offs = jnp.arange(-N_SIDE, N_SIDE + 1, dtype=jnp.int32)[None, :] * dils
    by_group = rel_bias.reshape(NUM_BUCKETS, N_ATT_GROUPS, HEADS_PER_GROUP)
    per_off = by_group[_t5_bucket(offs), jnp.arange(N_ATT_GROUPS)[:, None]]
    per_off = jnp.swapaxes(per_off, 1, 2).astype(_F32) * LOG2_E
    pad = jnp.full((N_ATT_GROUPS, HEADS_PER_GROUP, K_SUB - n_off), NEG_INF, _F32)
    return jnp.concatenate([per_off, pad], axis=2)


def _silu_of_half(h):
    return h + h * jnp.tanh(h)


def _epilogue_kernel(x_ref, za_ref, zprev_ref, znext_ref, zga_ref, zgb_ref, zbg_ref,
                     o1_ref, o4_ref, o16_ref, s1_ref, s4_ref, s16_ref,
                     icount_in, pscale_ref, bgate_ref, fgain_ref,
                     pmat_in, wpool_in, wpa_in, wpb_in, wout_in, expand_in,
                     out_ref,
                     ext_ref, o4n_ref, o16n_ref, s4n_ref, s16n_ref, icount_ref,
                     pmat_ref, wpool_ref, wpa_ref, wpb_ref, wout_ref, expand_ref, *, seq_len):
    tm = x_ref.shape[0]
    p0 = pl.program_id(1) * tm

    @pl.when(jnp.logical_and(pl.program_id(0) == 0, pl.program_id(1) == 0))
    def _stage_weights():
        for src, dst in ((pmat_in, pmat_ref), (wpa_in, wpa_ref), (wpb_in, wpb_ref),
                         (expand_in, expand_ref)):
            dst[...] = src[...].astype(_BF16)
        wout_ref[...] = (0.5 * wout_in[...]).astype(_BF16)
        wpool_ref[...] = jnp.zeros_like(wpool_ref)
        for gi in range(len(POOL_WINDOWS)):
            d0 = (gi % 2) * POOL_GROUP_W
            wpool_ref[gi // 2, d0:d0 + POOL_GROUP_W, d0:d0 + POOL_GROUP_W] = (
                wpool_in[gi].astype(_BF16))
        icount_ref[...] = jnp.broadcast_to(icount_in[0, 0:1, :], icount_ref.shape)

    icount_ref[0:HALO_ROWS] = jnp.where(p0 == 0, icount_in[1], icount_in[0])
    icount_ref[tm - HALO_ROWS:tm] = jnp.where(p0 + tm == seq_len, icount_in[2], icount_in[0])

    has_prev = (p0 > 0).astype(_BF16)
    has_next = (p0 + tm < seq_len).astype(_BF16)
    ext_ref[0:HALO_ROWS] = zprev_ref[...] * has_prev
    ext_ref[HALO_ROWS:HALO_ROWS + tm] = za_ref[:, 0:BRANCH_W]
    ext_ref[HALO_ROWS + tm:] = znext_ref[...] * has_next

    for dil, o_ref, on_ref, s_ref, sn_ref in ((4, o4_ref, o4n_ref, s4_ref, s4n_ref),
                                              (16, o16_ref, o16n_ref, s16_ref, s16n_ref)):
        rows = tm // dil
        for res in range(dil):
            o_res = o_ref[res].astype(_F32)
            for c in range(BRANCH_W // LANES):
                on_ref[c, pl.ds(res, rows, stride=dil), :] = o_res[:, c * LANES:(c + 1) * LANES]
            sn_ref[pl.ds(res, rows, stride=dil), :] = s_ref[res]

    n_chunks = tm // POOL_ROWS
    edge = [jnp.where(p0 == 0, 1, 0)] + [0] * (n_chunks - 2) + [jnp.where(p0 + tm == seq_len, 2, 0)]
    pooled = []
    for gi in range(len(POOL_WINDOWS)):
        cs = slice(gi * POOL_GROUP_W, (gi + 1) * POOL_GROUP_W)
        win = jnp.concatenate(
            [jnp.dot(pmat_ref[edge[ci], gi], ext_ref[ci * POOL_ROWS:ci * POOL_ROWS + POOL_K, cs],
                     preferred_element_type=_F32) for ci in range(n_chunks)], axis=0)
        pooled.append((win * icount_ref[:, cs]).astype(_BF16))
    mixed = [jnp.dot(jnp.concatenate(pooled[2 * pr:2 * pr + 2], axis=-1), wpool_ref[pr],
                     preferred_element_type=_F32) for pr in range(len(POOL_WINDOWS) // 2)]
    mixed = jnp.concatenate(mixed, axis=-1) * pscale_ref[...]
    ya_in = mixed * _silu_of_half(za_ref[:, BRANCH_W:].astype(_F32))
    y_a = jnp.dot(ya_in.astype(_BF16), wpa_ref[...], preferred_element_type=_F32)

    st = (s1_ref[...], s4n_ref[...], s16n_ref[...])
    mx = jnp.maximum(jnp.maximum(st[0], st[1]), st[2])
    es = [jnp.exp2(t - mx) for t in st]
    ls = [pltpu.roll(t, LANES - HEADS_PER_GROUP, 1) for t in st]
    head_lane = lax.broadcasted_iota(jnp.int32, (tm, LANES), 1) < HEADS_PER_GROUP
    den = jnp.where(head_lane, es[0] * ls[0] + es[1] * ls[1] + es[2] * ls[2], 1.0)
    inv = 1.0 / den
    packed = jnp.zeros((tm, LANES), _F32)
    for g, e in enumerate(es):
        wgt = jnp.where(head_lane, e * inv, 0.0)
        hi_part = wgt.astype(_BF16).astype(_F32)
        for part, lane0 in ((hi_part, HEADS_PER_GROUP * g),
                            (wgt - hi_part, HEADS_PER_GROUP * (N_ATT_GROUPS + g))):
            packed = packed + (pltpu.roll(part, lane0, 1) if lane0 else part)
    wide = jnp.dot(packed.astype(_BF16), expand_ref[...], preferred_element_type=_F32)
    ogs = (o1_ref[...].astype(_F32),
           jnp.concatenate([o4n_ref[c] for c in range(BRANCH_W // LANES)], axis=-1),
           jnp.concatenate([o16n_ref[c] for c in range(BRANCH_W // LANES)], axis=-1))
    att = sum(wide[:, g * BRANCH_W:(g + 1) * BRANCH_W] * og for g, og in enumerate(ogs))
    yb_in = att * _silu_of_half(zbg_ref[...].astype(_F32))
    y_b = jnp.dot(yb_in.astype(_BF16), wpb_ref[...], preferred_element_type=_F32)

    t_a = jnp.tanh(zga_ref[...].astype(_F32) + 0.5 * bgate_ref[0:1, :])
    t_b = jnp.tanh(zgb_ref[...].astype(_F32) + 0.5 * bgate_ref[1:2, :])
    merged2 = (y_a + y_b) + (t_a * y_a + t_b * y_b)
    xo = x_ref[...] + jnp.dot(merged2.astype(_BF16), wout_ref[...], preferred_element_type=_F32)
    r = lax.rsqrt(jnp.mean(xo * xo, axis=-1, keepdims=True) + RMS_EPS)
    out_ref[...] = (xo * r) * fgain_ref[...]


def _window_counts(seq_len, pos):
    pos = np.asarray(pos)[:, None]
    half = np.asarray([w // 2 for w in POOL_WINDOWS])[None, :]
    return np.minimum(pos + half - 1, seq_len - 1) - np.maximum(pos - half, 0) + 1


def _pool_matrices(seq_len):
    r = np.arange(POOL_ROWS)[:, None]
    c = np.arange(POOL_K)[None, :] - HALO_ROWS
    mid = seq_len // 2
    starts = (mid, 0, seq_len - POOL_ROWS)
    out = np.zeros((3, len(POOL_WINDOWS), POOL_ROWS, POOL_K), np.float32)
    for v, start in enumerate(starts):
        counts = _window_counts(seq_len, range(start, start + POOL_ROWS))
        for gi, w in enumerate(POOL_WINDOWS):
            window = (c - r >= -(w // 2)) & (c - r <= w // 2 - 1)
            out[v, gi] = window - counts[:, gi:gi + 1] * (c == r)
    return out


def _inv_window_counts(seq_len):
    mid = seq_len // 2
    rows = lambda start: np.repeat(
        1.0 / _window_counts(seq_len, range(start, start + HALO_ROWS)), POOL_GROUP_W, axis=1)
    return np.stack([rows(mid), rows(0), rows(seq_len - HALO_ROWS)]).astype(np.float32)


def _expand_matrix():
    row = np.arange(LANES)[:, None]
    col = np.arange(N_ATT_GROUPS * BRANCH_W)[None, :]
    n_packed = N_ATT_GROUPS * HEADS_PER_GROUP
    hit = ((row < 2 * n_packed)
           & ((row % n_packed) // HEADS_PER_GROUP == col // BRANCH_W)
           & (row % HEADS_PER_GROUP == (col % BRANCH_W) // HEAD_DIM))
    return hit.astype(np.float32)


def _epilogue(x, z_nat, outs, stats, w_pool, pool_scale, w_pa, w_pb, w_out, b_gate, final_gain):
    b, s, _ = x.shape
    tm = EPI_ROWS
    halo_per_tile = tm // HALO_ROWS
    last_halo = s // HALO_ROWS - 1
    pair_w = 2 * POOL_GROUP_W
    staged = ((jnp.asarray(_pool_matrices(s)[None], _BF16), None),
              (w_pool, (len(POOL_WINDOWS) // 2, pair_w, pair_w)),
              (w_pa, None), (w_pb, None), (w_out, None),
              (jnp.asarray(_expand_matrix()[None], _BF16), None))

    def full(a):
        return pl.BlockSpec(a.shape, lambda bi, i: (0,) * a.ndim)

    def layer0(a, **kw):
        return pl.BlockSpec((None,) + a.shape[1:], lambda bi, i: (0,) * a.ndim, **kw)

    def once(a):
        return layer0(a, pipeline_mode=pl.Buffered(1))

    def row(w, col):
        return pl.BlockSpec((None, tm, w), lambda bi, i: (bi, i, col))

    def by_residue(a):
        dil, w = a.shape[1], a.shape[3]
        return pl.BlockSpec((None, dil, tm // dil, w), lambda bi, i: (bi, 0, i, 0))

    in_specs = [
        row(D_MODEL, 0),
        row(2 * BRANCH_W, 0),
        pl.BlockSpec((None, HALO_ROWS, BRANCH_W),
                     lambda bi, i: (bi, jnp.maximum(i * halo_per_tile - 1, 0), 0)),
        pl.BlockSpec((None, HALO_ROWS, BRANCH_W),
                     lambda bi, i: (bi, jnp.minimum((i + 1) * halo_per_tile, last_halo), 0)),
        row(D_MODEL, 1), row(D_MODEL, 2),
        row(BRANCH_W, NAT_ATT_GATE),
        pl.BlockSpec((None, None, tm, BRANCH_W), lambda bi, i: (bi, 0, i, 0)),
        by_residue(outs[1]), by_residue(outs[2]),
        pl.BlockSpec((None, None, tm, LANES), lambda bi, i: (bi, 0, i, 0)),
        by_residue(stats[1]), by_residue(stats[2]),
        pl.BlockSpec((3, HALO_ROWS, BRANCH_W), lambda bi, i: (0, 0, 0)),
        full(pool_scale), layer0(b_gate), full(final_gain),
    ] + [once(a) for a, _ in staged]
    return pl.pallas_call(
        functools.partial(_epilogue_kernel, seq_len=s),
        grid=(b, s // tm),
        in_specs=in_specs,
        out_specs=pl.BlockSpec((None, tm, D_MODEL), lambda bi, i: (bi, i, 0)),
        out_shape=jax.ShapeDtypeStruct((b, s, D_MODEL), _F32),
        scratch_shapes=[pltpu.VMEM((tm + 2 * HALO_ROWS, BRANCH_W), _BF16),
                        pltpu.VMEM((BRANCH_W // LANES, tm, LANES), _F32),
                        pltpu.VMEM((BRANCH_W // LANES, tm, LANES), _F32),
                        pltpu.VMEM((tm, LANES), _F32), pltpu.VMEM((tm, LANES), _F32),
                        pltpu.VMEM((tm, BRANCH_W), _F32)]
        + [pltpu.VMEM(shape or a.shape[1:], _BF16) for a, shape in staged],
        compiler_params=pltpu.CompilerParams(
            dimension_semantics=("arbitrary", "arbitrary"), vmem_limit_bytes=VMEM_LIMIT),
        name="epilogue",
    )(x, z_nat, z_nat, z_nat, z_nat, z_nat, z_nat, *outs, *stats,
      jnp.asarray(_inv_window_counts(s)), pool_scale, b_gate, final_gain,
      *[a for a, _ in staged])


def _scaled_w_in(w):
    scale = np.ones((IN_W,), np.float32)
    scale[BRANCH_W:COL_Q] = 0.5
    scale[COL_Q:COL_K] = SCORE_SCALE
    scale[COL_ATT_GATE:] = 0.5
    return (w * scale).astype(_BF16)


def kernel(x, norm_gain, w_in, b_gate, rel_bias, w_pool, pool_scale, w_proj_a, w_proj_b,
           w_out, final_gain):
    b, s, d = x.shape
    assert d == D_MODEL and norm_gain.shape[0] == 1, "single-layer block of width D_MODEL"
    assert s % EPI_ROWS == 0 and s % IN_PROJ_ROWS == 0
    assert [dil for _, dil in DILATED_GROUPS] == [1, 4, 16]
    z_nat, z_d4, z_d16 = _in_proj(x, norm_gain, _scaled_w_in(w_in[0]))
    bias = _bias_per_offset(rel_bias)
    outs, stats = [], []
    for g, (zg, col0) in enumerate(((z_nat[:, None], NAT_QKV), (z_d4, 0), (z_d16, 0))):
        o, st = _attn_group(zg, bias, g, col0, *ATTN_STEP[g])
        outs.append(o)
        stats.append(st)
    return _epilogue(x, z_nat, outs, stats, w_pool, pool_scale, w_proj_a, w_proj_b, w_out,
                     b_gate, final_gain[None, :])
```

```python
import functools
import math

import jax
import jax.numpy as jnp
import numpy as np
from jax import lax
from jax.experimental import pallas as pl
from jax.experimental.pallas import tpu as pltpu

D_MODEL = 1024
BRANCH_W = D_MODEL // 2
POOL_WINDOWS = (2, 4, 8, 16)
POOL_GROUP_W = BRANCH_W // len(POOL_WINDOWS)
DILATED_GROUPS = ((128, 1), (512, 4), (2048, 16))
N_ATT_GROUPS = len(DILATED_GROUPS)
HEAD_DIM = 64
HEADS_PER_GROUP = BRANCH_W // HEAD_DIM
QKV_W = N_ATT_GROUPS * BRANCH_W
NUM_BUCKETS = 32
MAX_DISTANCE = 1024
IN_W = 2 * BRANCH_W + 3 * QKV_W + BRANCH_W + 2 * D_MODEL
RMS_EPS = 1e-6
NEG_INF = -1e30
LOG2_E = math.log2(math.e)
SCORE_SCALE = LOG2_E / math.sqrt(HEAD_DIM)

COL_Q = 2 * BRANCH_W
COL_K = COL_Q + QKV_W
COL_V = COL_K + QKV_W
COL_ATT_GATE = COL_V + QKV_W
COL_MERGE = COL_ATT_GATE + BRANCH_W

NAT_ATT_GATE = 6
NAT_QKV = 7
NAT_SRC_BLOCKS = (0, 1, *range(COL_MERGE // BRANCH_W, IN_W // BRANCH_W), COL_ATT_GATE // BRANCH_W,
                  COL_Q // BRANCH_W, COL_K // BRANCH_W, COL_V // BRANCH_W)
NAT_W = len(NAT_SRC_BLOCKS) * BRANCH_W
QKV_G_W = 3 * BRANCH_W

N_SIDE = 64
LANES = 128
Q_SUB = 128
K_SUB = Q_SUB + 2 * N_SIDE
ATTN_STEP = ((1, 2048), (1, 2048), (4, 512))
HALO_ROWS = 16
POOL_ROWS = 128
POOL_K = POOL_ROWS + 2 * HALO_ROWS

IN_PROJ_ROWS = 512
EPI_ROWS = 512
V7X_VMEM_BYTES = 64 * 1024 * 1024
VMEM_LIMIT = V7X_VMEM_BYTES * 7 // 8

_BF16 = jnp.bfloat16
_F32 = jnp.float32


def _in_proj_kernel(x_ref, g_ref, w_ref, zn_ref, z4_ref, z16_ref,
                    hf_ref, hn_ref, h4_ref, h16_ref):
    tm = x_ref.shape[0]
    x = x_ref[...]
    r = lax.rsqrt(jnp.mean(x * x, axis=-1, keepdims=True) + RMS_EPS)
    hf = (x * r) * g_ref[...]
    n_slab = D_MODEL // LANES
    for c in range(n_slab):
        hf_ref[c] = hf[:, c * LANES:(c + 1) * LANES]
    hn_ref[...] = hf.astype(_BF16)

    def project(h_ref, src_block):
        cs = slice(src_block * BRANCH_W, (src_block + 1) * BRANCH_W)
        return jnp.dot(h_ref[...], w_ref[:, cs], preferred_element_type=_F32).astype(_BF16)

    for dst_block, src_block in enumerate(NAT_SRC_BLOCKS):
        zn_ref[:, dst_block * BRANCH_W:(dst_block + 1) * BRANCH_W] = project(hn_ref, src_block)
    for g, dil, h_ref, z_ref in ((1, 4, h4_ref, z4_ref), (2, 16, h16_ref, z16_ref)):
        rows = tm // dil
        for res in range(dil):
            for c in range(n_slab):
                h_ref[res * rows:(res + 1) * rows, c * LANES:(c + 1) * LANES] = (
                    hf_ref[c, pl.ds(res, rows, stride=dil), :].astype(_BF16))
        for dst_block, col in enumerate((COL_Q, COL_K, COL_V)):
            zc = project(h_ref, col // BRANCH_W + g)
            for res in range(dil):
                z_ref[res, :, dst_block * BRANCH_W:(dst_block + 1) * BRANCH_W] = (
                    zc[res * rows:(res + 1) * rows])


def _in_proj(x, gain, w_all):
    b, s, _ = x.shape
    tm = IN_PROJ_ROWS
    return pl.pallas_call(
        _in_proj_kernel,
        grid=(b, s // tm),
        in_specs=[
            pl.BlockSpec((None, tm, D_MODEL), lambda bi, i: (bi, i, 0)),
            pl.BlockSpec((1, D_MODEL), lambda bi, i: (0, 0)),
            pl.BlockSpec(w_all.shape, lambda bi, i: (0, 0), pipeline_mode=pl.Buffered(1)),
        ],
        out_specs=[
            pl.BlockSpec((None, tm, NAT_W), lambda bi, i: (bi, i, 0)),
            pl.BlockSpec((None, 4, tm // 4, QKV_G_W), lambda bi, i: (bi, 0, i, 0)),
            pl.BlockSpec((None, 16, tm // 16, QKV_G_W), lambda bi, i: (bi, 0, i, 0)),
        ],
        out_shape=[
            jax.ShapeDtypeStruct((b, s, NAT_W), _BF16),
            jax.ShapeDtypeStruct((b, 4, s // 4, QKV_G_W), _BF16),
            jax.ShapeDtypeStruct((b, 16, s // 16, QKV_G_W), _BF16),
        ],
        scratch_shapes=[pltpu.VMEM((D_MODEL // LANES, tm, LANES), _F32),
                        pltpu.VMEM((tm, D_MODEL), _BF16),
                        pltpu.VMEM((tm, D_MODEL), _BF16), pltpu.VMEM((tm, D_MODEL), _BF16)],
        compiler_params=pltpu.CompilerParams(
            dimension_semantics=("arbitrary", "arbitrary"), vmem_limit_bytes=VMEM_LIMIT),
        name="in_proj",
    )(x, gain, w_all)


def _attn_kernel(q_ref, kp_ref, km_ref, kn_ref, vp_ref, vm_ref, vn_ref, per_off_ref,
                 o_ref, st_ref, bias_ref, *, n_steps):
    n_res, q_step = q_ref.shape[0], q_ref.shape[1]
    j = pl.program_id(2)

    @pl.when(jnp.logical_and(jnp.logical_and(pl.program_id(0) == 0, pl.program_id(1) == 0),
                             j == 0))
    def _build_bias():
        key = lax.broadcasted_iota(jnp.int32, (Q_SUB, K_SUB), 1)
        for h in range(HEADS_PER_GROUP):
            rows = slice((h % 2) * Q_SUB, (h % 2 + 1) * Q_SUB)
            band = pltpu.roll(jnp.broadcast_to(per_off_ref[h:h + 1, :], (Q_SUB, K_SUB)),
                              0, 1, stride=1, stride_axis=0)
            bias_ref[0, h // 2, rows, :] = band
            bias_ref[1, h // 2, rows, :] = jnp.where(key < N_SIDE, NEG_INF, band)
            bias_ref[2, h // 2, rows, :] = jnp.where(key >= K_SUB - N_SIDE, NEG_INF, band)

    lane = lax.broadcasted_iota(jnp.int32, (Q_SUB, LANES), 1)
    low_head = lane < HEAD_DIM
    n_sub = q_step // Q_SUB
    ones = jnp.ones((K_SUB, LANES), _BF16)

    def window(prev_ref, main_ref, next_ref, res, i, cs):
        if i == 0:
            return jnp.concatenate(
                [prev_ref[res, :, cs], main_ref[res, 0:K_SUB - N_SIDE, cs]], axis=0)
        if i == n_sub - 1:
            return jnp.concatenate(
                [main_ref[res, q_step - (K_SUB - N_SIDE):q_step, cs], next_ref[res, :, cs]], axis=0)
        return main_ref[res, i * Q_SUB - N_SIDE:i * Q_SUB - N_SIDE + K_SUB, cs]

    for res in range(n_res):
        for i in range(n_sub):
            r0 = i * Q_SUB
            if i == 0:
                var = jnp.where(j == 0, 1, 0)
            elif i == n_sub - 1:
                var = jnp.where(j == n_steps - 1, 2, 0)
            else:
                var = 0
            stats = jnp.zeros((Q_SUB, LANES), _F32)
            for hp in range(HEADS_PER_GROUP // 2):
                cs = slice(hp * LANES, (hp + 1) * LANES)
                q2 = q_ref[res, r0:r0 + Q_SUB, cs]
                k2 = window(kp_ref, km_ref, kn_ref, res, i, cs)
                v2 = jnp.concatenate([window(vp_ref, vm_ref, vn_ref, res, i, cs), ones], axis=1)
                zero = jnp.zeros_like(q2)
                qs = jnp.concatenate([jnp.where(low_head, q2, zero),
                                      jnp.where(low_head, zero, q2)], axis=0)
                s = lax.dot_general(qs, k2, (((1,), (1,)), ((), ())),
                                    preferred_element_type=_F32)
                s = s + bias_ref[var, hp]
                m = jnp.max(s, axis=-1, keepdims=True)
                p = jnp.exp2(s - m)
                ol = jnp.dot(p.astype(_BF16), v2, preferred_element_type=_F32)
                o, l = ol[:, :LANES], ol[:, LANES:]
                o_ref[res, r0:r0 + Q_SUB, cs] = (
                    jnp.where(low_head, o[:Q_SUB], o[Q_SUB:]).astype(_BF16))
                for a in range(2):
                    h = 2 * hp + a
                    rows = slice(a * Q_SUB, (a + 1) * Q_SUB)
                    stats = jnp.where(lane == h, m[rows], stats)
                    stats = jnp.where(lane == HEADS_PER_GROUP + h, l[rows], stats)
            st_ref[res, r0:r0 + Q_SUB, :] = stats


def _attn_group(zg, per_off, g, col0, n_res, q_step):
    b, dil, sp, _ = zg.shape
    assert dil % n_res == 0 and sp % q_step == 0 and q_step >= 2 * Q_SUB
    n_steps = sp // q_step
    hb = q_step // N_SIDE
    last_hb = sp // N_SIDE - 1

    def main(col, width=BRANCH_W):
        return pl.BlockSpec((None, n_res, q_step, width), lambda bi, r, j: (bi, r, j, col))

    def prev(col):
        return pl.BlockSpec((None, n_res, N_SIDE, BRANCH_W),
                            lambda bi, r, j: (bi, r, jnp.maximum(j * hb - 1, 0), col))

    def nxt(col):
        return pl.BlockSpec((None, n_res, N_SIDE, BRANCH_W),
                            lambda bi, r, j: (bi, r, jnp.minimum((j + 1) * hb, last_hb), col))

    qc, kc, vc = col0, col0 + 1, col0 + 2
    return pl.pallas_call(
        functools.partial(_attn_kernel, n_steps=n_steps),
        grid=(b, dil // n_res, n_steps),
        in_specs=[main(qc), prev(kc), main(kc), nxt(kc), prev(vc), main(vc), nxt(vc),
                  pl.BlockSpec((None,) + per_off.shape[1:], lambda bi, r, j: (g, 0, 0))],
        out_specs=[main(0), main(0, LANES)],
        out_shape=[
            jax.ShapeDtypeStruct((b, dil, sp, BRANCH_W), _BF16),
            jax.ShapeDtypeStruct((b, dil, sp, LANES), _F32),
        ],
        scratch_shapes=[pltpu.VMEM((3, HEADS_PER_GROUP // 2, 2 * Q_SUB, K_SUB), _F32)],
        compiler_params=pltpu.CompilerParams(
            dimension_semantics=("arbitrary", "arbitrary", "arbitrary"),
            vmem_limit_bytes=VMEM_LIMIT),
        name=f"attn_d{dil}",
    )(zg, zg, zg, zg, zg, zg, zg, per_off)


def _t5_bucket(rel):
    nb = NUM_BUCKETS // 2
    ret = (rel > 0).astype(jnp.int32) * nb
    n = jnp.abs(rel)
    max_exact = nb // 2
    nf = jnp.maximum(n, 1).astype(_F32)
    large = max_exact + (jnp.log(nf / max_exact) / math.log(MAX_DISTANCE / max_exact)
                         * (nb - max_exact)).astype(jnp.int32)
    large = jnp.minimum(large, nb - 1)
    return ret + jnp.where(n < max_exact, n, large)


def _bias_per_offset(rel_bias):
    n_off = 2 * N_SIDE + 1
    dils = jnp.asarray([dil for _, dil in DILATED_GROUPS], jnp.int32)[:, None]
    offs = jnp.arange(-N_SIDE, N_SIDE + 1, dtype=jnp.int32)[None, :] * dils
    by_group = rel_bias.reshape(NUM_BUCKETS, N_ATT_GROUPS, HEADS_PER_GROUP)
    per_off = by_group[_t5_bucket(offs), jnp.arange(N_ATT_GROUPS)[:, None]]
    per_off = jnp.swapaxes(per_off, 1, 2).astype(_F32) * LOG2_E
    pad = jnp.full((N_ATT_GROUPS, HEADS_PER_GROUP, K_SUB - n_off), NEG_INF, _F32)
    return jnp.concatenate([per_off, pad], axis=2)


def _silu_of_half(h):
    return h + h * jnp.tanh(h)


def _epilogue_kernel(x_ref, za_ref, zprev_ref, znext_ref, zga_ref, zgb_ref, zbg_ref,
                     o1_ref, o4_ref, o16_ref, s1_ref, s4_ref, s16_ref,
                     icount_in, pscale_ref, bgate_ref, fgain_ref,
                     pmat_in, wpool_in, wpa_in, wpb_in, wout_in,
                     out_ref,
                     ext_ref, o4n_ref, o16n_ref, s4n_ref, s16n_ref, icount_ref,
                     pmat_ref, wpool_ref, wpa_ref, wpb_ref, wout_ref, *, seq_len):
    tm = x_ref.shape[0]
    p0 = pl.program_id(1) * tm

    @pl.when(jnp.logical_and(pl.program_id(0) == 0, pl.program_id(1) == 0))
    def _stage_weights():
        for src, dst in ((pmat_in, pmat_ref), (wpa_in, wpa_ref), (wpb_in, wpb_ref)):
            dst[...] = src[...].astype(_BF16)
        wout_ref[...] = (0.5 * wout_in[...]).astype(_BF16)
        wpool_ref[...] = jnp.zeros_like(wpool_ref)
        for gi in range(len(POOL_WINDOWS)):
            d0 = (gi % 2) * POOL_GROUP_W
            wpool_ref[gi // 2, d0:d0 + POOL_GROUP_W, d0:d0 + POOL_GROUP_W] = (
                wpool_in[gi].astype(_BF16))
        icount_ref[...] = jnp.broadcast_to(icount_in[0, 0:1, :], icount_ref.shape)

    icount_ref[0:HALO_ROWS] = jnp.where(p0 == 0, icount_in[1], icount_in[0])
    icount_ref[tm - HALO_ROWS:tm] = jnp.where(p0 + tm == seq_len, icount_in[2], icount_in[0])

    has_prev = (p0 > 0).astype(_BF16)
    has_next = (p0 + tm < seq_len).astype(_BF16)
    ext_ref[0:HALO_ROWS] = zprev_ref[...] * has_prev
    ext_ref[HALO_ROWS:HALO_ROWS + tm] = za_ref[:, 0:BRANCH_W]
    ext_ref[HALO_ROWS + tm:] = znext_ref[...] * has_next

    for dil, o_ref, on_ref, s_ref, sn_ref in ((4, o4_ref, o4n_ref, s4_ref, s4n_ref),
                                              (16, o16_ref, o16n_ref, s16_ref, s16n_ref)):
        rows = tm // dil
        for res in range(dil):
            o_res = o_ref[res].astype(_F32)
            for c in range(BRANCH_W // LANES):
                on_ref[c, pl.ds(res, rows, stride=dil), :] = o_res[:, c * LANES:(c + 1) * LANES]
            sn_ref[pl.ds(res, rows, stride=dil), :] = s_ref[res]

    n_chunks = tm // POOL_ROWS
    edge = [jnp.where(p0 == 0, 1, 0)] + [0] * (n_chunks - 2) + [jnp.where(p0 + tm == seq_len, 2, 0)]
    pooled = []
    for gi in range(len(POOL_WINDOWS)):
        cs = slice(gi * POOL_GROUP_W, (gi + 1) * POOL_GROUP_W)
        win = jnp.concatenate(
            [jnp.dot(pmat_ref[edge[ci], gi], ext_ref[ci * POOL_ROWS:ci * POOL_ROWS + POOL_K, cs],
                     preferred_element_type=_F32) for ci in range(n_chunks)], axis=0)
        pooled.append((win * icount_ref[:, cs]).astype(_BF16))
    mixed = [jnp.dot(jnp.concatenate(pooled[2 * pr:2 * pr + 2], axis=-1), wpool_ref[pr],
                     preferred_element_type=_F32) for pr in range(len(POOL_WINDOWS) // 2)]
    mixed = jnp.concatenate(mixed, axis=-1) * pscale_ref[...]
    ya_in = mixed * _silu_of_half(za_ref[:, BRANCH_W:].astype(_F32))
    y_a = jnp.dot(ya_in.astype(_BF16), wpa_ref[...], preferred_element_type=_F32)

    st = (s1_ref[...], s4n_ref[...], s16n_ref[...])
    mx = jnp.maximum(jnp.maximum(st[0], st[1]), st[2])
    es = [jnp.exp2(t - mx) for t in st]
    ls = [pltpu.roll(t, LANES - HEADS_PER_GROUP, 1) for t in st]
    head_lane = lax.broadcasted_iota(jnp.int32, (tm, LANES), 1) < HEADS_PER_GROUP
    den = jnp.where(head_lane, es[0] * ls[0] + es[1] * ls[1] + es[2] * ls[2], 1.0)
    inv = 1.0 / den
    head_of_lane = lax.broadcasted_iota(jnp.int32, (tm, LANES), 1) // HEAD_DIM
    ogs = (o1_ref[...].astype(_F32),
           jnp.concatenate([o4n_ref[c] for c in range(BRANCH_W // LANES)], axis=-1),
           jnp.concatenate([o16n_ref[c] for c in range(BRANCH_W // LANES)], axis=-1))
    att = 0.0
    for e, og in zip(es, ogs):
        wgt = e * inv
        wide = jnp.concatenate(
            [jnp.take_along_axis(wgt, head_of_lane + c * (LANES // HEAD_DIM), axis=1)
             for c in range(BRANCH_W // LANES)], axis=-1)
        att = att + wide * og
    yb_in = att * _silu_of_half(zbg_ref[...].astype(_F32))
    y_b = jnp.dot(yb_in.astype(_BF16), wpb_ref[...], preferred_element_type=_F32)

    t_a = jnp.tanh(zga_ref[...].astype(_F32) + 0.5 * bgate_ref[0:1, :])
    t_b = jnp.tanh(zgb_ref[...].astype(_F32) + 0.5 * bgate_ref[1:2, :])
    merged2 = (y_a + y_b) + (t_a * y_a + t_b * y_b)
    xo = x_ref[...] + jnp.dot(merged2.astype(_BF16), wout_ref[...], preferred_element_type=_F32)
    r = lax.rsqrt(jnp.mean(xo * xo, axis=-1, keepdims=True) + RMS_EPS)
    out_ref[...] = (xo * r) * fgain_ref[...]


def _window_counts(seq_len, pos):
    pos = np.asarray(pos)[:, None]
    half = np.asarray([w // 2 for w in POOL_WINDOWS])[None, :]
    return np.minimum(pos + half - 1, seq_len - 1) - np.maximum(pos - half, 0) + 1


def _pool_matrices(seq_len):
    r = np.arange(POOL_ROWS)[:, None]
    c = np.arange(POOL_K)[None, :] - HALO_ROWS
    mid = seq_len // 2
    starts = (mid, 0, seq_len - POOL_ROWS)
    out = np.zeros((3, len(POOL_WINDOWS), POOL_ROWS, POOL_K), np.float32)
    for v, start in enumerate(starts):
        counts = _window_counts(seq_len, range(start, start + POOL_ROWS))
        for gi, w in enumerate(POOL_WINDOWS):
            window = (c - r >= -(w // 2)) & (c - r <= w // 2 - 1)
            out[v, gi] = window - counts[:, gi:gi + 1] * (c == r)
    return out


def _inv_window_counts(seq_len):
    mid = seq_len // 2
    rows = lambda start: np.repeat(
        1.0 / _window_counts(seq_len, range(start, start + HALO_ROWS)), POOL_GROUP_W, axis=1)
    return np.stack([rows(mid), rows(0), rows(seq_len - HALO_ROWS)]).astype(np.float32)


def _epilogue(x, z_nat, outs, stats, w_pool, pool_scale, w_pa, w_pb, w_out, b_gate, final_gain):
    b, s, _ = x.shape
    tm = EPI_ROWS
    halo_per_tile = tm // HALO_ROWS
    last_halo = s // HALO_ROWS - 1
    pair_w = 2 * POOL_GROUP_W
    staged = ((jnp.asarray(_pool_matrices(s)[None], _BF16), None),
              (w_pool, (len(POOL_WINDOWS) // 2, pair_w, pair_w)),
              (w_pa, None), (w_pb, None), (w_out, None))

    def full(a):
        return pl.BlockSpec(a.shape, lambda bi, i: (0,) * a.ndim)

    def layer0(a, **kw):
        return pl.BlockSpec((None,) + a.shape[1:], lambda bi, i: (0,) * a.ndim, **kw)

    def once(a):
        return layer0(a, pipeline_mode=pl.Buffered(1))

    def row(w, col):
        return pl.BlockSpec((None, tm, w), lambda bi, i: (bi, i, col))

    def by_residue(a):
        dil, w = a.shape[1], a.shape[3]
        return pl.BlockSpec((None, dil, tm // dil, w), lambda bi, i: (bi, 0, i, 0))

    in_specs = [
        row(D_MODEL, 0),
        row(2 * BRANCH_W, 0),
        pl.BlockSpec((None, HALO_ROWS, BRANCH_W),
                     lambda bi, i: (bi, jnp.maximum(i * halo_per_tile - 1, 0), 0)),
        pl.BlockSpec((None, HALO_ROWS, BRANCH_W),
                     lambda bi, i: (bi, jnp.minimum((i + 1) * halo_per_tile, last_halo), 0)),
        row(D_MODEL, 1), row(D_MODEL, 2),
        row(BRANCH_W, NAT_ATT_GATE),
        pl.BlockSpec((None, None, tm, BRANCH_W), lambda bi, i: (bi, 0, i, 0)),
        by_residue(outs[1]), by_residue(outs[2]),
        pl.BlockSpec((None, None, tm, LANES), lambda bi, i: (bi, 0, i, 0)),
        by_residue(stats[1]), by_residue(stats[2]),
        pl.BlockSpec((3, HALO_ROWS, BRANCH_W), lambda bi, i: (0, 0, 0)),
        full(pool_scale), layer0(b_gate), full(final_gain),
    ] + [once(a) for a, _ in staged]
    return pl.pallas_call(
        functools.partial(_epilogue_kernel, seq_len=s),
        grid=(b, s // tm),
        in_specs=in_specs,
        out_specs=pl.BlockSpec((None, tm, D_MODEL), lambda bi, i: (bi, i, 0)),
        out_shape=jax.ShapeDtypeStruct((b, s, D_MODEL), _F32),
        scratch_shapes=[pltpu.VMEM((tm + 2 * HALO_ROWS, BRANCH_W), _BF16),
                        pltpu.VMEM((BRANCH_W // LANES, tm, LANES), _F32),
                        pltpu.VMEM((BRANCH_W // LANES, tm, LANES), _F32),
                        pltpu.VMEM((tm, LANES), _F32), pltpu.VMEM((tm, LANES), _F32),
                        pltpu.VMEM((tm, BRANCH_W), _F32)]
        + [pltpu.VMEM(shape or a.shape[1:], _BF16) for a, shape in staged],
        compiler_params=pltpu.CompilerParams(
            dimension_semantics=("arbitrary", "arbitrary"), vmem_limit_bytes=VMEM_LIMIT),
        name="epilogue",
    )(x, z_nat, z_nat, z_nat, z_nat, z_nat, z_nat, *outs, *stats,
      jnp.asarray(_inv_window_counts(s)), pool_scale, b_gate, final_gain,
      *[a for a, _ in staged])


def _scaled_w_in(w):
    scale = np.ones((IN_W,), np.float32)
    scale[BRANCH_W:COL_Q] = 0.5
    scale[COL_Q:COL_K] = SCORE_SCALE
    scale[COL_ATT_GATE:] = 0.5
    return (w * scale).astype(_BF16)


def kernel(x, norm_gain, w_in, b_gate, rel_bias, w_pool, pool_scale, w_proj_a, w_proj_b,
           w_out, final_gain):
    b, s, d = x.shape
    assert d == D_MODEL and norm_gain.shape[0] == 1, "single-layer block of width D_MODEL"
    assert s % EPI_ROWS == 0 and s % IN_PROJ_ROWS == 0
    assert [dil for _, dil in DILATED_GROUPS] == [1, 4, 16]
    z_nat, z_d4, z_d16 = _in_proj(x, norm_gain, _scaled_w_in(w_in[0]))
    bias = _bias_per_offset(rel_bias)
    outs, stats = [], []
    for g, (zg, col0) in enumerate(((z_nat[:, None], NAT_QKV), (z_d4, 0), (z_d16, 0))):
        o, st = _attn_group(zg, bias, g, col0, *ATTN_STEP[g])
        outs.append(o)
        stats.append(st)
    return _epilogue(x, z_nat, outs, stats, w_pool, pool_scale, w_proj_a, w_proj_b, w_out,
                     b_gate, final_gain[None, :])
```

```python
import functools
import math

import jax
import jax.numpy as jnp
import numpy as np
from jax import lax
from jax.experimental import pallas as pl
from jax.experimental.pallas import tpu as pltpu

D_MODEL = 1024
BRANCH_W = D_MODEL // 2
POOL_WINDOWS = (2, 4, 8, 16)
POOL_GROUP_W = BRANCH_W // len(POOL_WINDOWS)
DILATED_GROUPS = ((128, 1), (512, 4), (2048, 16))
N_ATT_GROUPS = len(DILATED_GROUPS)
HEAD_DIM = 64
HEADS_PER_GROUP = BRANCH_W // HEAD_DIM
QKV_W = N_ATT_GROUPS * BRANCH_W
NUM_BUCKETS = 32
MAX_DISTANCE = 1024
IN_W = 2 * BRANCH_W + 3 * QKV_W + BRANCH_W + 2 * D_MODEL
RMS_EPS = 1e-6
NEG_INF = -1e30
LOG2_E = math.log2(math.e)
SCORE_SCALE = LOG2_E / math.sqrt(HEAD_DIM)

COL_Q = 2 * BRANCH_W
COL_K = COL_Q + QKV_W
COL_V = COL_K + QKV_W
COL_ATT_GATE = COL_V + QKV_W
COL_MERGE = COL_ATT_GATE + BRANCH_W

NAT_ATT_GATE = 6
NAT_QKV = 7
NAT_SRC_BLOCKS = (0, 1, *range(COL_MERGE // BRANCH_W, IN_W // BRANCH_W), COL_ATT_GATE // BRANCH_W,
                  COL_Q // BRANCH_W, COL_K // BRANCH_W, COL_V // BRANCH_W)
NAT_W = len(NAT_SRC_BLOCKS) * BRANCH_W
QKV_G_W = 3 * BRANCH_W

N_SIDE = 64
LANES = 128
Q_SUB = 128
K_SUB = Q_SUB + 2 * N_SIDE
ATTN_STEP = ((1, 2048), (1, 2048), (4, 512))
HALO_ROWS = 16
POOL_ROWS = 128
POOL_K = POOL_ROWS + 2 * HALO_ROWS

IN_PROJ_ROWS = 512
EPI_ROWS = 512
MERGE_SLAB = 512
V7X_VMEM_BYTES = 64 * 1024 * 1024
VMEM_LIMIT = V7X_VMEM_BYTES * 7 // 8

_BF16 = jnp.bfloat16
_F32 = jnp.float32


def _in_proj_kernel(x_ref, g_ref, w_ref, zn_ref, z4_ref, z16_ref,
                    hf_ref, hn_ref, h4_ref, h16_ref):
    tm = x_ref.shape[0]
    x = x_ref[...]
    r = lax.rsqrt(jnp.mean(x * x, axis=-1, keepdims=True) + RMS_EPS)
    hf = (x * r) * g_ref[...]
    n_slab = D_MODEL // LANES
    for c in range(n_slab):
        hf_ref[c] = hf[:, c * LANES:(c + 1) * LANES]
    hn_ref[...] = hf.astype(_BF16)

    def project(h_ref, src_block):
        cs = slice(src_block * BRANCH_W, (src_block + 1) * BRANCH_W)
        return jnp.dot(h_ref[...], w_ref[:, cs], preferred_element_type=_F32).astype(_BF16)

    for dst_block, src_block in enumerate(NAT_SRC_BLOCKS):
        zn_ref[:, dst_block * BRANCH_W:(dst_block + 1) * BRANCH_W] = project(hn_ref, src_block)
    for g, dil, h_ref, z_ref in ((1, 4, h4_ref, z4_ref), (2, 16, h16_ref, z16_ref)):
        rows = tm // dil
        for res in range(dil):
            for c in range(n_slab):
                h_ref[res * rows:(res + 1) * rows, c * LANES:(c + 1) * LANES] = (
                    hf_ref[c, pl.ds(res, rows, stride=dil), :].astype(_BF16))
        for dst_block, col in enumerate((COL_Q, COL_K, COL_V)):
            zc = project(h_ref, col // BRANCH_W + g)
            for res in range(dil):
                z_ref[res, :, dst_block * BRANCH_W:(dst_block + 1) * BRANCH_W] = (
                    zc[res * rows:(res + 1) * rows])


def _in_proj(x, gain, w_all):
    b, s, _ = x.shape
    tm = IN_PROJ_ROWS
    return pl.pallas_call(
        _in_proj_kernel,
        grid=(b, s // tm),
        in_specs=[
            pl.BlockSpec((None, tm, D_MODEL), lambda bi, i: (bi, i, 0)),
            pl.BlockSpec((1, D_MODEL), lambda bi, i: (0, 0)),
            pl.BlockSpec(w_all.shape, lambda bi, i: (0, 0), pipeline_mode=pl.Buffered(1)),
        ],
        out_specs=[
            pl.BlockSpec((None, tm, NAT_W), lambda bi, i: (bi, i, 0)),
            pl.BlockSpec((None, 4, tm // 4, QKV_G_W), lambda bi, i: (bi, 0, i, 0)),
            pl.BlockSpec((None, 16, tm // 16, QKV_G_W), lambda bi, i: (bi, 0, i, 0)),
        ],
        out_shape=[
            jax.ShapeDtypeStruct((b, s, NAT_W), _BF16),
            jax.ShapeDtypeStruct((b, 4, s // 4, QKV_G_W), _BF16),
            jax.ShapeDtypeStruct((b, 16, s // 16, QKV_G_W), _BF16),
        ],
        scratch_shapes=[pltpu.VMEM((D_MODEL // LANES, tm, LANES), _F32),
                        pltpu.VMEM((tm, D_MODEL), _BF16),
                        pltpu.VMEM((tm, D_MODEL), _BF16), pltpu.VMEM((tm, D_MODEL), _BF16)],
        compiler_params=pltpu.CompilerParams(
            dimension_semantics=("arbitrary", "arbitrary"), vmem_limit_bytes=VMEM_LIMIT),
        name="in_proj",
    )(x, gain, w_all)


def _attn_kernel(q_ref, kp_ref, km_ref, kn_ref, vp_ref, vm_ref, vn_ref, per_off_ref,
                 o_ref, st_ref, bias_ref, *, n_steps):
    n_res, q_step = q_ref.shape[0], q_ref.shape[1]
    j = pl.program_id(2)

    @pl.when(jnp.logical_and(jnp.logical_and(pl.program_id(0) == 0, pl.program_id(1) == 0),
                             j == 0))
    def _build_bias():
        key = lax.broadcasted_iota(jnp.int32, (Q_SUB, K_SUB), 1)
        for h in range(HEADS_PER_GROUP):
            rows = slice((h % 2) * Q_SUB, (h % 2 + 1) * Q_SUB)
            band = pltpu.roll(jnp.broadcast_to(per_off_ref[h:h + 1, :], (Q_SUB, K_SUB)),
                              0, 1, stride=1, stride_axis=0)
            bias_ref[0, h // 2, rows, :] = band
            bias_ref[1, h // 2, rows, :] = jnp.where(key < N_SIDE, NEG_INF, band)
            bias_ref[2, h // 2, rows, :] = jnp.where(key >= K_SUB - N_SIDE, NEG_INF, band)

    lane = lax.broadcasted_iota(jnp.int32, (Q_SUB, LANES), 1)
    low_head = lane < HEAD_DIM
    n_sub = q_step // Q_SUB
    ones = jnp.ones((K_SUB, LANES), _BF16)

    def window(prev_ref, main_ref, next_ref, res, i, cs):
        if i == 0:
            return jnp.concatenate(
                [prev_ref[res, :, cs], main_ref[res, 0:K_SUB - N_SIDE, cs]], axis=0)
        if i == n_sub - 1:
            return jnp.concatenate(
                [main_ref[res, q_step - (K_SUB - N_SIDE):q_step, cs], next_ref[res, :, cs]], axis=0)
        return main_ref[res, i * Q_SUB - N_SIDE:i * Q_SUB - N_SIDE + K_SUB, cs]

    for res in range(n_res):
        for i in range(n_sub):
            r0 = i * Q_SUB
            if i == 0:
                var = jnp.where(j == 0, 1, 0)
            elif i == n_sub - 1:
                var = jnp.where(j == n_steps - 1, 2, 0)
            else:
                var = 0
            stats = jnp.zeros((Q_SUB, LANES), _F32)
            for hp in range(HEADS_PER_GROUP // 2):
                cs = slice(hp * LANES, (hp + 1) * LANES)
                q2 = q_ref[res, r0:r0 + Q_SUB, cs]
                k2 = window(kp_ref, km_ref, kn_ref, res, i, cs)
                v2 = jnp.concatenate([window(vp_ref, vm_ref, vn_ref, res, i, cs), ones], axis=1)
                zero = jnp.zeros_like(q2)
                qs = jnp.concatenate([jnp.where(low_head, q2, zero),
                                      jnp.where(low_head, zero, q2)], axis=0)
                s = lax.dot_general(qs, k2, (((1,), (1,)), ((), ())),
                                    preferred_element_type=_F32)
                s = s + bias_ref[var, hp]
                m = jnp.max(s, axis=-1, keepdims=True)
                p = jnp.exp2(s - m)
                ol = jnp.dot(p.astype(_BF16), v2, preferred_element_type=_F32)
                o, l = ol[:, :LANES], ol[:, LANES:]
                o_ref[res, r0:r0 + Q_SUB, cs] = (
                    jnp.where(low_head, o[:Q_SUB], o[Q_SUB:]).astype(_BF16))
                for a in range(2):
                    h = 2 * hp + a
                    rows = slice(a * Q_SUB, (a + 1) * Q_SUB)
                    stats = jnp.where(lane == h, m[rows], stats)
                    stats = jnp.where(lane == HEADS_PER_GROUP + h, l[rows], stats)
            st_ref[res, r0:r0 + Q_SUB, :] = stats


def _attn_group(zg, per_off, g, col0, n_res, q_step):
    b, dil, sp, _ = zg.shape
    assert dil % n_res == 0 and sp % q_step == 0 and q_step >= 2 * Q_SUB
    n_steps = sp // q_step
    hb = q_step // N_SIDE
    last_hb = sp // N_SIDE - 1

    def main(col, width=BRANCH_W):
        return pl.BlockSpec((None, n_res, q_step, width), lambda bi, r, j: (bi, r, j, col))

    def prev(col):
        return pl.BlockSpec((None, n_res, N_SIDE, BRANCH_W),
                            lambda bi, r, j: (bi, r, jnp.maximum(j * hb - 1, 0), col))

    def nxt(col):
        return pl.BlockSpec((None, n_res, N_SIDE, BRANCH_W),
                            lambda bi, r, j: (bi, r, jnp.minimum((j + 1) * hb, last_hb), col))

    qc, kc, vc = col0, col0 + 1, col0 + 2
    return pl.pallas_call(
        functools.partial(_attn_kernel, n_steps=n_steps),
        grid=(b, dil // n_res, n_steps),
        in_specs=[main(qc), prev(kc), main(kc), nxt(kc), prev(vc), main(vc), nxt(vc),
                  pl.BlockSpec((None,) + per_off.shape[1:], lambda bi, r, j: (g, 0, 0))],
        out_specs=[main(0), main(0, LANES)],
        out_shape=[
            jax.ShapeDtypeStruct((b, dil, sp, BRANCH_W), _BF16),
            jax.ShapeDtypeStruct((b, dil, sp, LANES), _F32),
        ],
        scratch_shapes=[pltpu.VMEM((3, HEADS_PER_GROUP // 2, 2 * Q_SUB, K_SUB), _F32)],
        compiler_params=pltpu.CompilerParams(
            dimension_semantics=("arbitrary", "arbitrary", "arbitrary"),
            vmem_limit_bytes=VMEM_LIMIT),
        name=f"attn_d{dil}",
    )(zg, zg, zg, zg, zg, zg, zg, per_off)


def _t5_bucket(rel):
    nb = NUM_BUCKETS // 2
    ret = (rel > 0).astype(jnp.int32) * nb
    n = jnp.abs(rel)
    max_exact = nb // 2
    nf = jnp.maximum(n, 1).astype(_F32)
    large = max_exact + (jnp.log(nf / max_exact) / math.log(MAX_DISTANCE / max_exact)
                         * (nb - max_exact)).astype(jnp.int32)
    large = jnp.minimum(large, nb - 1)
    return ret + jnp.where(n < max_exact, n, large)


def _bias_per_offset(rel_bias):
    n_off = 2 * N_SIDE + 1
    dils = jnp.asarray([dil for _, dil in DILATED_GROUPS], jnp.int32)[:, None]
    offs = jnp.arange(-N_SIDE, N_SIDE + 1, dtype=jnp.int32)[None, :] * dils
    by_group = rel_bias.reshape(NUM_BUCKETS, N_ATT_GROUPS, HEADS_PER_GROUP)
    per_off = by_group[_t5_bucket(offs), jnp.arange(N_ATT_GROUPS)[:, None]]
    per_off = jnp.swapaxes(per_off, 1, 2).astype(_F32) * LOG2_E
    pad = jnp.full((N_ATT_GROUPS, HEADS_PER_GROUP, K_SUB - n_off), NEG_INF, _F32)
    return jnp.concatenate([per_off, pad], axis=2)


def _silu_of_half(h):
    return h + h * jnp.tanh(h)


def _epilogue_kernel(x_ref, za_ref, zprev_ref, znext_ref, zga_ref, zgb_ref, zbg_ref,
                     o1_ref, o4_ref, o16_ref, s1_ref, s4_ref, s16_ref,
                     icount_in, pscale_ref, bgate_ref, fgain_ref,
                     pmat_in, wpool_in, wpa_in, wpb_in, wout_in,
                     out_ref,
                     ext_ref, o4n_ref, o16n_ref, s4n_ref, s16n_ref, icount_ref,
                     pmat_ref, wpool_ref, wpa_ref, wpb_ref, wout_ref, *, seq_len):
    tm = x_ref.shape[0]
    p0 = pl.program_id(1) * tm

    @pl.when(jnp.logical_and(pl.program_id(0) == 0, pl.program_id(1) == 0))
    def _stage_weights():
        for src, dst in ((pmat_in, pmat_ref), (wpa_in, wpa_ref), (wpb_in, wpb_ref)):
            dst[...] = src[...].astype(_BF16)
        wout_ref[...] = (0.5 * wout_in[...]).astype(_BF16)
        wpool_ref[...] = jnp.zeros_like(wpool_ref)
        for gi in range(len(POOL_WINDOWS)):
            d0 = (gi % 2) * POOL_GROUP_W
            wpool_ref[gi // 2, d0:d0 + POOL_GROUP_W, d0:d0 + POOL_GROUP_W] = (
                wpool_in[gi].astype(_BF16))
        icount_ref[...] = jnp.broadcast_to(icount_in[0, 0:1, :], icount_ref.shape)

    icount_ref[0:HALO_ROWS] = jnp.where(p0 == 0, icount_in[1], icount_in[0])
    icount_ref[tm - HALO_ROWS:tm] = jnp.where(p0 + tm == seq_len, icount_in[2], icount_in[0])

    has_prev = (p0 > 0).astype(_BF16)
    has_next = (p0 + tm < seq_len).astype(_BF16)
    ext_ref[0:HALO_ROWS] = zprev_ref[...] * has_prev
    ext_ref[HALO_ROWS:HALO_ROWS + tm] = za_ref[:, 0:BRANCH_W]
    ext_ref[HALO_ROWS + tm:] = znext_ref[...] * has_next

    for dil, o_ref, on_ref, s_ref, sn_ref in ((4, o4_ref, o4n_ref, s4_ref, s4n_ref),
                                              (16, o16_ref, o16n_ref, s16_ref, s16n_ref)):
        rows = tm // dil
        for res in range(dil):
            o_res = o_ref[res].astype(_F32)
            for c in range(BRANCH_W // LANES):
                on_ref[c, pl.ds(res, rows, stride=dil), :] = o_res[:, c * LANES:(c + 1) * LANES]
            sn_ref[pl.ds(res, rows, stride=dil), :] = s_ref[res]

    n_chunks = tm // POOL_ROWS
    edge = [jnp.where(p0 == 0, 1, 0)] + [0] * (n_chunks - 2) + [jnp.where(p0 + tm == seq_len, 2, 0)]
    pooled = []
    for gi in range(len(POOL_WINDOWS)):
        cs = slice(gi * POOL_GROUP_W, (gi + 1) * POOL_GROUP_W)
        win = jnp.concatenate(
            [jnp.dot(pmat_ref[edge[ci], gi], ext_ref[ci * POOL_ROWS:ci * POOL_ROWS + POOL_K, cs],
                     preferred_element_type=_F32) for ci in range(n_chunks)], axis=0)
        pooled.append((win * icount_ref[:, cs]).astype(_BF16))
    mixed = [jnp.dot(jnp.concatenate(pooled[2 * pr:2 * pr + 2], axis=-1), wpool_ref[pr],
                     preferred_element_type=_F32) for pr in range(len(POOL_WINDOWS) // 2)]
    mixed = jnp.concatenate(mixed, axis=-1) * pscale_ref[...]
    ya_in = mixed * _silu_of_half(za_ref[:, BRANCH_W:].astype(_F32))
    ya_in = ya_in.astype(_BF16)

    st = (s1_ref[...], s4n_ref[...], s16n_ref[...])
    mx = jnp.maximum(jnp.maximum(st[0], st[1]), st[2])
    es = [jnp.exp2(t - mx) for t in st]
    ls = [pltpu.roll(t, LANES - HEADS_PER_GROUP, 1) for t in st]
    head_lane = lax.broadcasted_iota(jnp.int32, (tm, LANES), 1) < HEADS_PER_GROUP
    den = jnp.where(head_lane, es[0] * ls[0] + es[1] * ls[1] + es[2] * ls[2], 1.0)
    inv = 1.0 / den
    head_of_lane = lax.broadcasted_iota(jnp.int32, (tm, LANES), 1) // HEAD_DIM
    ogs = (o1_ref[...].astype(_F32),
           jnp.concatenate([o4n_ref[c] for c in range(BRANCH_W // LANES)], axis=-1),
           jnp.concatenate([o16n_ref[c] for c in range(BRANCH_W // LANES)], axis=-1))
    att = 0.0
    for e, og in zip(es, ogs):
        wgt = e * inv
        wide = jnp.concatenate(
            [jnp.take_along_axis(wgt, head_of_lane + c * (LANES // HEAD_DIM), axis=1)
             for c in range(BRANCH_W // LANES)], axis=-1)
        att = att + wide * og
    yb_in = att * _silu_of_half(zbg_ref[...].astype(_F32))
    yb_in = yb_in.astype(_BF16)

    xo = x_ref[...]
    for c0 in range(0, D_MODEL, MERGE_SLAB):
        cs = slice(c0, c0 + MERGE_SLAB)
        y_a = jnp.dot(ya_in, wpa_ref[:, cs], preferred_element_type=_F32)
        y_b = jnp.dot(yb_in, wpb_ref[:, cs], preferred_element_type=_F32)
        t_a = jnp.tanh(zga_ref[:, cs].astype(_F32) + 0.5 * bgate_ref[0:1, cs])
        t_b = jnp.tanh(zgb_ref[:, cs].astype(_F32) + 0.5 * bgate_ref[1:2, cs])
        merged2 = (y_a + y_b) + (t_a * y_a + t_b * y_b)
        xo = xo + jnp.dot(merged2.astype(_BF16), wout_ref[cs, :], preferred_element_type=_F32)
    r = lax.rsqrt(jnp.mean(xo * xo, axis=-1, keepdims=True) + RMS_EPS)
    out_ref[...] = (xo * r) * fgain_ref[...]


def _window_counts(seq_len, pos):
    pos = np.asarray(pos)[:, None]
    half = np.asarray([w // 2 for w in POOL_WINDOWS])[None, :]
    return np.minimum(pos + half - 1, seq_len - 1) - np.maximum(pos - half, 0) + 1


def _pool_matrices(seq_len):
    r = np.arange(POOL_ROWS)[:, None]
    c = np.arange(POOL_K)[None, :] - HALO_ROWS
    mid = seq_len // 2
    starts = (mid, 0, seq_len - POOL_ROWS)
    out = np.zeros((3, len(POOL_WINDOWS), POOL_ROWS, POOL_K), np.float32)
    for v, start in enumerate(starts):
        counts = _window_counts(seq_len, range(start, start + POOL_ROWS))
        for gi, w in enumerate(POOL_WINDOWS):
            window = (c - r >= -(w // 2)) & (c - r <= w // 2 - 1)
            out[v, gi] = window - counts[:, gi:gi + 1] * (c == r)
    return out


def _inv_window_counts(seq_len):
    mid = seq_len // 2
    rows = lambda start: np.repeat(
        1.0 / _window_counts(seq_len, range(start, start + HALO_ROWS)), POOL_GROUP_W, axis=1)
    return np.stack([rows(mid), rows(0), rows(seq_len - HALO_ROWS)]).astype(np.float32)


def _epilogue(x, z_nat, outs, stats, w_pool, pool_scale, w_pa, w_pb, w_out, b_gate, final_gain):
    b, s, _ = x.shape
    tm = EPI_ROWS
    halo_per_tile = tm // HALO_ROWS
    last_halo = s // HALO_ROWS - 1
    pair_w = 2 * POOL_GROUP_W
    staged = ((jnp.asarray(_pool_matrices(s)[None], _BF16), None),
              (w_pool, (len(POOL_WINDOWS) // 2, pair_w, pair_w)),
              (w_pa, None), (w_pb, None), (w_out, None))

    def full(a):
        return pl.BlockSpec(a.shape, lambda bi, i: (0,) * a.ndim)

    def layer0(a, **kw):
        return pl.BlockSpec((None,) + a.shape[1:], lambda bi, i: (0,) * a.ndim, **kw)

    def once(a):
        return layer0(a, pipeline_mode=pl.Buffered(1))

    def row(w, col):
        return pl.BlockSpec((None, tm, w), lambda bi, i: (bi, i, col))

    def by_residue(a):
        dil, w = a.shape[1], a.shape[3]
        return pl.BlockSpec((None, dil, tm // dil, w), lambda bi, i: (bi, 0, i, 0))

    in_specs = [
        row(D_MODEL, 0),
        row(2 * BRANCH_W, 0),
        pl.BlockSpec((None, HALO_ROWS, BRANCH_W),
                     lambda bi, i: (bi, jnp.maximum(i * halo_per_tile - 1, 0), 0)),
        pl.BlockSpec((None, HALO_ROWS, BRANCH_W),
                     lambda bi, i: (bi, jnp.minimum((i + 1) * halo_per_tile, last_halo), 0)),
        row(D_MODEL, 1), row(D_MODEL, 2),
        row(BRANCH_W, NAT_ATT_GATE),
        pl.BlockSpec((None, None, tm, BRANCH_W), lambda bi, i: (bi, 0, i, 0)),
        by_residue(outs[1]), by_residue(outs[2]),
        pl.BlockSpec((None, None, tm, LANES), lambda bi, i: (bi, 0, i, 0)),
        by_residue(stats[1]), by_residue(stats[2]),
        pl.BlockSpec((3, HALO_ROWS, BRANCH_W), lambda bi, i: (0, 0, 0)),
        full(pool_scale), layer0(b_gate), full(final_gain),
    ] + [once(a) for a, _ in staged]
    return pl.pallas_call(
        functools.partial(_epilogue_kernel, seq_len=s),
        grid=(b, s // tm),
        in_specs=in_specs,
        out_specs=pl.BlockSpec((None, tm, D_MODEL), lambda bi, i: (bi, i, 0)),
        out_shape=jax.ShapeDtypeStruct((b, s, D_MODEL), _F32),
        scratch_shapes=[pltpu.VMEM((tm + 2 * HALO_ROWS, BRANCH_W), _BF16),
                        pltpu.VMEM((BRANCH_W // LANES, tm, LANES), _F32),
                        pltpu.VMEM((BRANCH_W // LANES, tm, LANES), _F32),
                        pltpu.VMEM((tm, LANES), _F32), pltpu.VMEM((tm, LANES), _F32),
                        pltpu.VMEM((tm, BRANCH_W), _F32)]
        + [pltpu.VMEM(shape or a.shape[1:], _BF16) for a, shape in staged],
        compiler_params=pltpu.CompilerParams(
            dimension_semantics=("arbitrary", "arbitrary"), vmem_limit_bytes=VMEM_LIMIT),
        name="epilogue",
    )(x, z_nat, z_nat, z_nat, z_nat, z_nat, z_nat, *outs, *stats,
      jnp.asarray(_inv_window_counts(s)), pool_scale, b_gate, final_gain,
      *[a for a, _ in staged])


def _scaled_w_in(w):
    scale = np.ones((IN_W,), np.float32)
    scale[BRANCH_W:COL_Q] = 0.5
    scale[COL_Q:COL_K] = SCORE_SCALE
    scale[COL_ATT_GATE:] = 0.5
    return (w * scale).astype(_BF16)


def kernel(x, norm_gain, w_in, b_gate, rel_bias, w_pool, pool_scale, w_proj_a, w_proj_b,
           w_out, final_gain):
    b, s, d = x.shape
    assert d == D_MODEL and norm_gain.shape[0] == 1, "single-layer block of width D_MODEL"
    assert s % EPI_ROWS == 0 and s % IN_PROJ_ROWS == 0
    assert [dil for _, dil in DILATED_GROUPS] == [1, 4, 16]
    z_nat, z_d4, z_d16 = _in_proj(x, norm_gain, _scaled_w_in(w_in[0]))
    bias = _bias_per_offset(rel_bias)
    outs, stats = [], []
    for g, (zg, col0) in enumerate(((z_nat[:, None], NAT_QKV), (z_d4, 0), (z_d16, 0))):
        o, st = _attn_group(zg, bias, g, col0, *ATTN_STEP[g])
        outs.append(o)
        stats.append(st)
    return _epilogue(x, z_nat, outs, stats, w_pool, pool_scale, w_proj_a, w_proj_b, w_out,
                     b_gate, final_gain[None, :])
```

```python
import functools
import math

import jax
import jax.numpy as jnp
import numpy as np
from jax import lax
from jax.experimental import pallas as pl
from jax.experimental.pallas import tpu as pltpu

D_MODEL = 1024
BRANCH_W = D_MODEL // 2
POOL_WINDOWS = (2, 4, 8, 16)
POOL_GROUP_W = BRANCH_W // len(POOL_WINDOWS)
DILATED_GROUPS = ((128, 1), (512, 4), (2048, 16))
N_ATT_GROUPS = len(DILATED_GROUPS)
HEAD_DIM = 64
HEADS_PER_GROUP = BRANCH_W // HEAD_DIM
QKV_W = N_ATT_GROUPS * BRANCH_W
NUM_BUCKETS = 32
MAX_DISTANCE = 1024
IN_W = 2 * BRANCH_W + 3 * QKV_W + BRANCH_W + 2 * D_MODEL
RMS_EPS = 1e-6
NEG_INF = -1e30
LOG2_E = math.log2(math.e)
SCORE_SCALE = LOG2_E / math.sqrt(HEAD_DIM)

COL_Q = 2 * BRANCH_W
COL_K = COL_Q + QKV_W
COL_V = COL_K + QKV_W
COL_ATT_GATE = COL_V + QKV_W
COL_MERGE = COL_ATT_GATE + BRANCH_W

NAT_ATT_GATE = 6
NAT_SRC_BLOCKS = (0, 1, *range(COL_MERGE // BRANCH_W, IN_W // BRANCH_W), COL_ATT_GATE // BRANCH_W)
NAT_W = len(NAT_SRC_BLOCKS) * BRANCH_W
QKV_G_W = 3 * BRANCH_W

N_SIDE = 64
LANES = 128
Q_SUB = 128
K_SUB = Q_SUB + 2 * N_SIDE
ATTN_STEP = ((1, 2048), (1, 2048), (4, 512))
HALO_ROWS = 16
POOL_ROWS = 128
POOL_K = POOL_ROWS + 2 * HALO_ROWS

IN_PROJ_ROWS = 512
EPI_ROWS = 512
V7X_VMEM_BYTES = 64 * 1024 * 1024
VMEM_LIMIT = V7X_VMEM_BYTES * 7 // 8

_BF16 = jnp.bfloat16
_F32 = jnp.float32


def _in_proj_kernel(x_ref, g_ref, w_ref, zn_ref, z1_ref, z4_ref, z16_ref,
                    hf_ref, hn_ref, h4_ref, h16_ref):
    tm = x_ref.shape[0]
    x = x_ref[...]
    r = lax.rsqrt(jnp.mean(x * x, axis=-1, keepdims=True) + RMS_EPS)
    hf = (x * r) * g_ref[...]
    n_slab = D_MODEL // LANES
    for c in range(n_slab):
        hf_ref[c] = hf[:, c * LANES:(c + 1) * LANES]
    hn_ref[...] = hf.astype(_BF16)

    def project(h_ref, src_block):
        cs = slice(src_block * BRANCH_W, (src_block + 1) * BRANCH_W)
        return jnp.dot(h_ref[...], w_ref[:, cs], preferred_element_type=_F32).astype(_BF16)

    for dst_block, src_block in enumerate(NAT_SRC_BLOCKS):
        zn_ref[:, dst_block * BRANCH_W:(dst_block + 1) * BRANCH_W] = project(hn_ref, src_block)
    for dst_block, col in enumerate((COL_Q, COL_K, COL_V)):
        z1_ref[:, dst_block * BRANCH_W:(dst_block + 1) * BRANCH_W] = (
            project(hn_ref, col // BRANCH_W))
    for g, dil, h_ref, z_ref in ((1, 4, h4_ref, z4_ref), (2, 16, h16_ref, z16_ref)):
        rows = tm // dil
        for res in range(dil):
            for c in range(n_slab):
                h_ref[res * rows:(res + 1) * rows, c * LANES:(c + 1) * LANES] = (
                    hf_ref[c, pl.ds(res, rows, stride=dil), :].astype(_BF16))
        for dst_block, col in enumerate((COL_Q, COL_K, COL_V)):
            zc = project(h_ref, col // BRANCH_W + g)
            for res in range(dil):
                z_ref[res, :, dst_block * BRANCH_W:(dst_block + 1) * BRANCH_W] = (
                    zc[res * rows:(res + 1) * rows])


def _in_proj(x, gain, w_all):
    b, s, _ = x.shape
    tm = IN_PROJ_ROWS
    return pl.pallas_call(
        _in_proj_kernel,
        grid=(b, s // tm),
        in_specs=[
            pl.BlockSpec((None, tm, D_MODEL), lambda bi, i: (bi, i, 0)),
            pl.BlockSpec((1, D_MODEL), lambda bi, i: (0, 0)),
            pl.BlockSpec(w_all.shape, lambda bi, i: (0, 0), pipeline_mode=pl.Buffered(1)),
        ],
        out_specs=[
            pl.BlockSpec((None, tm, NAT_W), lambda bi, i: (bi, i, 0)),
            pl.BlockSpec((None, tm, QKV_G_W), lambda bi, i: (bi, i, 0)),
            pl.BlockSpec((None, 4, tm // 4, QKV_G_W), lambda bi, i: (bi, 0, i, 0)),
            pl.BlockSpec((None, 16, tm // 16, QKV_G_W), lambda bi, i: (bi, 0, i, 0)),
        ],
        out_shape=[
            jax.ShapeDtypeStruct((b, s, NAT_W), _BF16),
            jax.ShapeDtypeStruct((b, s, QKV_G_W), _BF16),
            jax.ShapeDtypeStruct((b, 4, s // 4, QKV_G_W), _BF16),
            jax.ShapeDtypeStruct((b, 16, s // 16, QKV_G_W), _BF16),
        ],
        scratch_shapes=[pltpu.VMEM((D_MODEL // LANES, tm, LANES), _F32),
                        pltpu.VMEM((tm, D_MODEL), _BF16),
                        pltpu.VMEM((tm, D_MODEL), _BF16), pltpu.VMEM((tm, D_MODEL), _BF16)],
        compiler_params=pltpu.CompilerParams(
            dimension_semantics=("arbitrary", "arbitrary"), vmem_limit_bytes=VMEM_LIMIT),
        name="in_proj",
    )(x, gain, w_all)


def _attn_kernel(q_ref, kp_ref, km_ref, kn_ref, vp_ref, vm_ref, vn_ref, per_off_ref,
                 o_ref, st_ref, bias_ref, *, n_steps):
    n_res, q_step = q_ref.shape[0], q_ref.shape[1]
    j = pl.program_id(2)

    @pl.when(jnp.logical_and(jnp.logical_and(pl.program_id(0) == 0, pl.program_id(1) == 0),
                             j == 0))
    def _build_bias():
        key = lax.broadcasted_iota(jnp.int32, (Q_SUB, K_SUB), 1)
        for h in range(HEADS_PER_GROUP):
            rows = slice((h % 2) * Q_SUB, (h % 2 + 1) * Q_SUB)
            band = pltpu.roll(jnp.broadcast_to(per_off_ref[h:h + 1, :], (Q_SUB, K_SUB)),
                              0, 1, stride=1, stride_axis=0)
            bias_ref[0, h // 2, rows, :] = band
            bias_ref[1, h // 2, rows, :] = jnp.where(key < N_SIDE, NEG_INF, band)
            bias_ref[2, h // 2, rows, :] = jnp.where(key >= K_SUB - N_SIDE, NEG_INF, band)

    lane = lax.broadcasted_iota(jnp.int32, (Q_SUB, LANES), 1)
    low_head = lane < HEAD_DIM
    n_sub = q_step // Q_SUB
    ones = jnp.ones((K_SUB, LANES), _BF16)

    def window(prev_ref, main_ref, next_ref, res, i, cs):
        if i == 0:
            return jnp.concatenate(
                [prev_ref[res, :, cs], main_ref[res, 0:K_SUB - N_SIDE, cs]], axis=0)
        if i == n_sub - 1:
            return jnp.concatenate(
                [main_ref[res, q_step - (K_SUB - N_SIDE):q_step, cs], next_ref[res, :, cs]], axis=0)
        return main_ref[res, i * Q_SUB - N_SIDE:i * Q_SUB - N_SIDE + K_SUB, cs]

    for res in range(n_res):
        for i in range(n_sub):
            r0 = i * Q_SUB
            if i == 0:
                var = jnp.where(j == 0, 1, 0)
            elif i == n_sub - 1:
                var = jnp.where(j == n_steps - 1, 2, 0)
            else:
                var = 0
            stats = jnp.zeros((Q_SUB, LANES), _F32)
            for hp in range(HEADS_PER_GROUP // 2):
                cs = slice(hp * LANES, (hp + 1) * LANES)
                q2 = q_ref[res, r0:r0 + Q_SUB, cs]
                k2 = window(kp_ref, km_ref, kn_ref, res, i, cs)
                v2 = jnp.concatenate([window(vp_ref, vm_ref, vn_ref, res, i, cs), ones], axis=1)
                zero = jnp.zeros_like(q2)
                qs = jnp.concatenate([jnp.where(low_head, q2, zero),
                                      jnp.where(low_head, zero, q2)], axis=0)
                s = lax.dot_general(qs, k2, (((1,), (1,)), ((), ())),
                                    preferred_element_type=_F32)
                s = s + bias_ref[var, hp]
                m = jnp.max(s, axis=-1, keepdims=True)
                p = jnp.exp2(s - m)
                ol = jnp.dot(p.astype(_BF16), v2, preferred_element_type=_F32)
                o, l = ol[:, :LANES], ol[:, LANES:]
                o_ref[res, r0:r0 + Q_SUB, cs] = (
                    jnp.where(low_head, o[:Q_SUB], o[Q_SUB:]).astype(_BF16))
                for a in range(2):
                    h = 2 * hp + a
                    rows = slice(a * Q_SUB, (a + 1) * Q_SUB)
                    stats = jnp.where(lane == h, m[rows], stats)
                    stats = jnp.where(lane == HEADS_PER_GROUP + h, l[rows], stats)
            st_ref[res, r0:r0 + Q_SUB, :] = stats


def _attn_group(zg, per_off, g, col0, n_res, q_step):
    b, dil, sp, _ = zg.shape
    assert dil % n_res == 0 and sp % q_step == 0 and q_step >= 2 * Q_SUB
    n_steps = sp // q_step
    hb = q_step // N_SIDE
    last_hb = sp // N_SIDE - 1

    def main(col, width=BRANCH_W):
        return pl.BlockSpec((None, n_res, q_step, width), lambda bi, r, j: (bi, r, j, col))

    def prev(col):
        return pl.BlockSpec((None, n_res, N_SIDE, BRANCH_W),
                            lambda bi, r, j: (bi, r, jnp.maximum(j * hb - 1, 0), col))

    def nxt(col):
        return pl.BlockSpec((None, n_res, N_SIDE, BRANCH_W),
                            lambda bi, r, j: (bi, r, jnp.minimum((j + 1) * hb, last_hb), col))

    qc, kc, vc = col0, col0 + 1, col0 + 2
    return pl.pallas_call(
        functools.partial(_attn_kernel, n_steps=n_steps),
        grid=(b, dil // n_res, n_steps),
        in_specs=[main(qc), prev(kc), main(kc), nxt(kc), prev(vc), main(vc), nxt(vc),
                  pl.BlockSpec((None,) + per_off.shape[1:], lambda bi, r, j: (g, 0, 0))],
        out_specs=[main(0), main(0, LANES)],
        out_shape=[
            jax.ShapeDtypeStruct((b, dil, sp, BRANCH_W), _BF16),
            jax.ShapeDtypeStruct((b, dil, sp, LANES), _F32),
        ],
        scratch_shapes=[pltpu.VMEM((3, HEADS_PER_GROUP // 2, 2 * Q_SUB, K_SUB), _F32)],
        compiler_params=pltpu.CompilerParams(
            dimension_semantics=("arbitrary", "arbitrary", "arbitrary"),
            vmem_limit_bytes=VMEM_LIMIT),
        name=f"attn_d{dil}",
    )(zg, zg, zg, zg, zg, zg, zg, per_off)


def _t5_bucket(rel):
    nb = NUM_BUCKETS // 2
    ret = (rel > 0).astype(jnp.int32) * nb
    n = jnp.abs(rel)
    max_exact = nb // 2
    nf = jnp.maximum(n, 1).astype(_F32)
    large = max_exact + (jnp.log(nf / max_exact) / math.log(MAX_DISTANCE / max_exact)
                         * (nb - max_exact)).astype(jnp.int32)
    large = jnp.minimum(large, nb - 1)
    return ret + jnp.where(n < max_exact, n, large)


def _bias_per_offset(rel_bias):
    n_off = 2 * N_SIDE + 1
    dils = jnp.asarray([dil for _, dil in DILATED_GROUPS], jnp.int32)[:, None]
    offs = jnp.arange(-N_SIDE, N_SIDE + 1, dtype=jnp.int32)[None, :] * dils
    by_group = rel_bias.reshape(NUM_BUCKETS, N_ATT_GROUPS, HEADS_PER_GROUP)
    per_off = by_group[_t5_bucket(offs), jnp.arange(N_ATT_GROUPS)[:, None]]
    per_off = jnp.swapaxes(per_off, 1, 2).astype(_F32) * LOG2_E
    pad = jnp.full((N_ATT_GROUPS, HEADS_PER_GROUP, K_SUB - n_off), NEG_INF, _F32)
    return jnp.concatenate([per_off, pad], axis=2)


def _silu_of_half(h):
    return h + h * jnp.tanh(h)


def _epilogue_kernel(x_ref, za_ref, zprev_ref, znext_ref, zga_ref, zgb_ref, zbg_ref,
                     o1_ref, o4_ref, o16_ref, s1_ref, s4_ref, s16_ref,
                     icount_in, pscale_ref, bgate_ref, fgain_ref,
                     pmat_in, wpool_in, wpa_in, wpb_in, wout_in,
                     out_ref,
                     ext_ref, o4n_ref, o16n_ref, s4n_ref, s16n_ref, icount_ref,
                     pmat_ref, wpool_ref, wpa_ref, wpb_ref, wout_ref, *, seq_len):
    tm = x_ref.shape[0]
    p0 = pl.program_id(1) * tm

    @pl.when(jnp.logical_and(pl.program_id(0) == 0, pl.program_id(1) == 0))
    def _stage_weights():
        for src, dst in ((pmat_in, pmat_ref), (wpa_in, wpa_ref), (wpb_in, wpb_ref)):
            dst[...] = src[...].astype(_BF16)
        wout_ref[...] = (0.5 * wout_in[...]).astype(_BF16)
        wpool_ref[...] = jnp.zeros_like(wpool_ref)
        for gi in range(len(POOL_WINDOWS)):
            d0 = (gi % 2) * POOL_GROUP_W
            wpool_ref[gi // 2, d0:d0 + POOL_GROUP_W, d0:d0 + POOL_GROUP_W] = (
                wpool_in[gi].astype(_BF16))
        icount_ref[...] = jnp.broadcast_to(icount_in[0, 0:1, :], icount_ref.shape)

    icount_ref[0:HALO_ROWS] = jnp.where(p0 == 0, icount_in[1], icount_in[0])
    icount_ref[tm - HALO_ROWS:tm] = jnp.where(p0 + tm == seq_len, icount_in[2], icount_in[0])

    has_prev = (p0 > 0).astype(_BF16)
    has_next = (p0 + tm < seq_len).astype(_BF16)
    ext_ref[0:HALO_ROWS] = zprev_ref[...] * has_prev
    ext_ref[HALO_ROWS:HALO_ROWS + tm] = za_ref[:, 0:BRANCH_W]
    ext_ref[HALO_ROWS + tm:] = znext_ref[...] * has_next

    for dil, o_ref, on_ref, s_ref, sn_ref in ((4, o4_ref, o4n_ref, s4_ref, s4n_ref),
                                              (16, o16_ref, o16n_ref, s16_ref, s16n_ref)):
        rows = tm // dil
        for res in range(dil):
            o_res = o_ref[res].astype(_F32)
            for c in range(BRANCH_W // LANES):
                on_ref[c, pl.ds(res, rows, stride=dil), :] = o_res[:, c * LANES:(c + 1) * LANES]
            sn_ref[pl.ds(res, rows, stride=dil), :] = s_ref[res]

    n_chunks = tm // POOL_ROWS
    edge = [jnp.where(p0 == 0, 1, 0)] + [0] * (n_chunks - 2) + [jnp.where(p0 + tm == seq_len, 2, 0)]
    pooled = []
    for gi in range(len(POOL_WINDOWS)):
        cs = slice(gi * POOL_GROUP_W, (gi + 1) * POOL_GROUP_W)
        win = jnp.concatenate(
            [jnp.dot(pmat_ref[edge[ci], gi], ext_ref[ci * POOL_ROWS:ci * POOL_ROWS + POOL_K, cs],
                     preferred_element_type=_F32) for ci in range(n_chunks)], axis=0)
        pooled.append((win * icount_ref[:, cs]).astype(_BF16))
    mixed = [jnp.dot(jnp.concatenate(pooled[2 * pr:2 * pr + 2], axis=-1), wpool_ref[pr],
                     preferred_element_type=_F32) for pr in range(len(POOL_WINDOWS) // 2)]
    mixed = jnp.concatenate(mixed, axis=-1) * pscale_ref[...]
    ya_in = mixed * _silu_of_half(za_ref[:, BRANCH_W:].astype(_F32))
    y_a = jnp.dot(ya_in.astype(_BF16), wpa_ref[...], preferred_element_type=_F32)

    st = (s1_ref[...], s4n_ref[...], s16n_ref[...])
    mx = jnp.maximum(jnp.maximum(st[0], st[1]), st[2])
    es = [jnp.exp2(t - mx) for t in st]
    ls = [pltpu.roll(t, LANES - HEADS_PER_GROUP, 1) for t in st]
    head_lane = lax.broadcasted_iota(jnp.int32, (tm, LANES), 1) < HEADS_PER_GROUP
    den = jnp.where(head_lane, es[0] * ls[0] + es[1] * ls[1] + es[2] * ls[2], 1.0)
    inv = 1.0 / den
    head_of_lane = lax.broadcasted_iota(jnp.int32, (tm, LANES), 1) // HEAD_DIM
    ogs = (o1_ref[...].astype(_F32),
           jnp.concatenate([o4n_ref[c] for c in range(BRANCH_W // LANES)], axis=-1),
           jnp.concatenate([o16n_ref[c] for c in range(BRANCH_W // LANES)], axis=-1))
    att = 0.0
    for e, og in zip(es, ogs):
        wgt = e * inv
        wide = jnp.concatenate(
            [jnp.take_along_axis(wgt, head_of_lane + c * (LANES // HEAD_DIM), axis=1)
             for c in range(BRANCH_W // LANES)], axis=-1)
        att = att + wide * og
    yb_in = att * _silu_of_half(zbg_ref[...].astype(_F32))
    y_b = jnp.dot(yb_in.astype(_BF16), wpb_ref[...], preferred_element_type=_F32)

    t_a = jnp.tanh(zga_ref[...].astype(_F32) + 0.5 * bgate_ref[0:1, :])
    t_b = jnp.tanh(zgb_ref[...].astype(_F32) + 0.5 * bgate_ref[1:2, :])
    merged2 = (y_a + y_b) + (t_a * y_a + t_b * y_b)
    xo = x_ref[...] + jnp.dot(merged2.astype(_BF16), wout_ref[...], preferred_element_type=_F32)
    r = lax.rsqrt(jnp.mean(xo * xo, axis=-1, keepdims=True) + RMS_EPS)
    out_ref[...] = (xo * r) * fgain_ref[...]


def _window_counts(seq_len, pos):
    pos = np.asarray(pos)[:, None]
    half = np.asarray([w // 2 for w in POOL_WINDOWS])[None, :]
    return np.minimum(pos + half - 1, seq_len - 1) - np.maximum(pos - half, 0) + 1


def _pool_matrices(seq_len):
    r = np.arange(POOL_ROWS)[:, None]
    c = np.arange(POOL_K)[None, :] - HALO_ROWS
    mid = seq_len // 2
    starts = (mid, 0, seq_len - POOL_ROWS)
    out = np.zeros((3, len(POOL_WINDOWS), POOL_ROWS, POOL_K), np.float32)
    for v, start in enumerate(starts):
        counts = _window_counts(seq_len, range(start, start + POOL_ROWS))
        for gi, w in enumerate(POOL_WINDOWS):
            window = (c - r >= -(w // 2)) & (c - r <= w // 2 - 1)
            out[v, gi] = window - counts[:, gi:gi + 1] * (c == r)
    return out


def _inv_window_counts(seq_len):
    mid = seq_len // 2
    rows = lambda start: np.repeat(
        1.0 / _window_counts(seq_len, range(start, start + HALO_ROWS)), POOL_GROUP_W, axis=1)
    return np.stack([rows(mid), rows(0), rows(seq_len - HALO_ROWS)]).astype(np.float32)


def _epilogue(x, z_nat, outs, stats, w_pool, pool_scale, w_pa, w_pb, w_out, b_gate, final_gain):
    b, s, _ = x.shape
    tm = EPI_ROWS
    halo_per_tile = tm // HALO_ROWS
    last_halo = s // HALO_ROWS - 1
    pair_w = 2 * POOL_GROUP_W
    staged = ((jnp.asarray(_pool_matrices(s)[None], _BF16), None),
              (w_pool, (len(POOL_WINDOWS) // 2, pair_w, pair_w)),
              (w_pa, None), (w_pb, None), (w_out, None))

    def full(a):
        return pl.BlockSpec(a.shape, lambda bi, i: (0,) * a.ndim)

    def layer0(a, **kw):
        return pl.BlockSpec((None,) + a.shape[1:], lambda bi, i: (0,) * a.ndim, **kw)

    def once(a):
        return layer0(a, pipeline_mode=pl.Buffered(1))

    def row(w, col):
        return pl.BlockSpec((None, tm, w), lambda bi, i: (bi, i, col))

    def by_residue(a):
        dil, w = a.shape[1], a.shape[3]
        return pl.BlockSpec((None, dil, tm // dil, w), lambda bi, i: (bi, 0, i, 0))

    in_specs = [
        row(D_MODEL, 0),
        row(2 * BRANCH_W, 0),
        pl.BlockSpec((None, HALO_ROWS, BRANCH_W),
                     lambda bi, i: (bi, jnp.maximum(i * halo_per_tile - 1, 0), 0)),
        pl.BlockSpec((None, HALO_ROWS, BRANCH_W),
                     lambda bi, i: (bi, jnp.minimum((i + 1) * halo_per_tile, last_halo), 0)),
        row(D_MODEL, 1), row(D_MODEL, 2),
        row(BRANCH_W, NAT_ATT_GATE),
        pl.BlockSpec((None, None, tm, BRANCH_W), lambda bi, i: (bi, 0, i, 0)),
        by_residue(outs[1]), by_residue(outs[2]),
        pl.BlockSpec((None, None, tm, LANES), lambda bi, i: (bi, 0, i, 0)),
        by_residue(stats[1]), by_residue(stats[2]),
        pl.BlockSpec((3, HALO_ROWS, BRANCH_W), lambda bi, i: (0, 0, 0)),
        full(pool_scale), layer0(b_gate), full(final_gain),
    ] + [once(a) for a, _ in staged]
    return pl.pallas_call(
        functools.partial(_epilogue_kernel, seq_len=s),
        grid=(b, s // tm),
        in_specs=in_specs,
        out_specs=pl.BlockSpec((None, tm, D_MODEL), lambda bi, i: (bi, i, 0)),
        out_shape=jax.ShapeDtypeStruct((b, s, D_MODEL), _F32),
        scratch_shapes=[pltpu.VMEM((tm + 2 * HALO_ROWS, BRANCH_W), _BF16),
                        pltpu.VMEM((BRANCH_W // LANES, tm, LANES), _F32),
                        pltpu.VMEM((BRANCH_W // LANES, tm, LANES), _F32),
                        pltpu.VMEM((tm, LANES), _F32), pltpu.VMEM((tm, LANES), _F32),
                        pltpu.VMEM((tm, BRANCH_W), _F32)]
        + [pltpu.VMEM(shape or a.shape[1:], _BF16) for a, shape in staged],
        compiler_params=pltpu.CompilerParams(
            dimension_semantics=("arbitrary", "arbitrary"), vmem_limit_bytes=VMEM_LIMIT),
        name="epilogue",
    )(x, z_nat, z_nat, z_nat, z_nat, z_nat, z_nat, *outs, *stats,
      jnp.asarray(_inv_window_counts(s)), pool_scale, b_gate, final_gain,
      *[a for a, _ in staged])


def _scaled_w_in(w):
    scale = np.ones((IN_W,), np.float32)
    scale[BRANCH_W:COL_Q] = 0.5
    scale[COL_Q:COL_K] = SCORE_SCALE
    scale[COL_ATT_GATE:] = 0.5
    return (w * scale).astype(_BF16)


def kernel(x, norm_gain, w_in, b_gate, rel_bias, w_pool, pool_scale, w_proj_a, w_proj_b,
           w_out, final_gain):
    b, s, d = x.shape
    assert d == D_MODEL and norm_gain.shape[0] == 1, "single-layer block of width D_MODEL"
    assert s % EPI_ROWS == 0 and s % IN_PROJ_ROWS == 0
    assert [dil for _, dil in DILATED_GROUPS] == [1, 4, 16]
    z_nat, z_d1, z_d4, z_d16 = _in_proj(x, norm_gain, _scaled_w_in(w_in[0]))
    bias = _bias_per_offset(rel_bias)
    outs, stats = [], []
    for g, zg in enumerate((z_d1[:, None], z_d4, z_d16)):
        o, st = _attn_group(zg, bias, g, 0, *ATTN_STEP[g])
        outs.append(o)
        stats.append(st)
    return _epilogue(x, z_nat, outs, stats, w_pool, pool_scale, w_proj_a, w_proj_b, w_out,
                     b_gate, final_gain[None, :])
```
